```python
import jax
import jax.numpy as jnp
from jax import lax
import numpy as np

D_MODEL = 2048
BATCH = 1
SEQ = 8192
DEPTH = 4

CHUNK = 64
D_CONV = D_MODEL // 2
D_ATT = D_MODEL // 2
D_MIX = D_CONV + D_ATT
HEAD_DIM = 128
N_HEADS = D_ATT // HEAD_DIM
CONV_WIDTH = 3
Q_BLOCK = 128
N_EXPERTS = 64
TOP_K = 8
N_GROUPS = 8
TOPK_GROUPS = 4
D_EXPERT = 384
ROUTED_SCALE = 2.5
EXPERT_BLOCK = 256
DEEPNORM_ALPHA = (2 * DEPTH) ** 0.25
DEEPNORM_BETA = (8 * DEPTH) ** -0.25
LN_EPS = 1e-5
RMS_EPS = 1e-6
N_MOD = 6

kernel_name = "hybrid_conv_stickbreaking_moe_deepnorm_adaln"


def layer_norm(x, g, b):
    xf = x.astype(jnp.float32)
    mu = jnp.mean(xf, axis=-1, keepdims=True)
    xc = xf - mu
    var = jnp.mean(xc * xc, axis=-1, keepdims=True)
    y = xc * lax.rsqrt(var + LN_EPS) * g.astype(jnp.float32) + b.astype(jnp.float32)
    return y.astype(x.dtype)


def rms_norm(x, g):
    xf = x.astype(jnp.float32)
    y = xf * lax.rsqrt(jnp.mean(xf * xf, axis=-1, keepdims=True) + RMS_EPS) * g.astype(jnp.float32)
    return y.astype(x.dtype)


def short_gated_conv(h, b_gate, c_gate, conv_w):
    s = h.shape[1]
    v = c_gate * h
    vp = jnp.pad(v, ((0, 0), (CONV_WIDTH - 1, 0), (0, 0)))
    conv = sum(vp[:, k:k + s, :] * conv_w[k] for k in range(CONV_WIDTH))
    return b_gate * conv


def stick_breaking_attention(q, k, v):
    bsz, s, _ = q.shape
    n_blk = s // Q_BLOCK
    q = q.reshape(bsz, s, N_HEADS, HEAD_DIM).transpose(0, 2, 1, 3)
    kf = k.reshape(bsz, s, N_HEADS, HEAD_DIM).transpose(0, 2, 1, 3).astype(jnp.float32)
    vf = v.reshape(bsz, s, N_HEADS, HEAD_DIM).transpose(0, 2, 1, 3).astype(jnp.float32)
    q_blocks = q.reshape(bsz, N_HEADS, n_blk, Q_BLOCK, HEAD_DIM).transpose(2, 0, 1, 3, 4)
    s_pos = jnp.arange(s, dtype=jnp.int32)
    scale = HEAD_DIM ** -0.5

    def one_block(args):
        i, qi = args
        z = jnp.einsum('bhqd,bhkd->bhqk', qi.astype(jnp.float32), kf) * scale
        t_pos = i * Q_BLOCK + jnp.arange(Q_BLOCK, dtype=jnp.int32)
        mask = s_pos[None, :] < t_pos[:, None]
        log_keep = jnp.where(mask, jax.nn.log_sigmoid(-z), 0.0)
        later = lax.cumsum(log_keep, axis=3, reverse=True) - log_keep
        a = jnp.where(mask, jnp.exp(jax.nn.log_sigmoid(z) + later), 0.0)
        return jnp.einsum('bhqk,bhkd->bhqd', a, vf)

    o = lax.map(one_block, (jnp.arange(n_blk, dtype=jnp.int32), q_blocks))
    o = o.transpose(1, 0, 3, 2, 4).reshape(bsz, s, D_ATT)
    return o.astype(v.dtype)


def hybrid_mixer(u, w_in, conv_w, g_conv, g_att, w_out):
    proj = jnp.einsum('bsd,de->bse', u, w_in)
    cuts = [D_CONV, 2 * D_CONV, 3 * D_CONV, 3 * D_CONV + D_ATT, 3 * D_CONV + 2 * D_ATT]
    h, b_gate, c_gate, q, k, v = jnp.split(proj, cuts, axis=-1)
    y_conv = short_gated_conv(h, b_gate, c_gate, conv_w)
    y_att = stick_breaking_attention(q, k, v)
    y = jnp.concatenate([rms_norm(y_conv, g_conv), rms_norm(y_att, g_att)], axis=-1)
    return jnp.einsum('bse,ed->bsd', y, w_out)


def swiglu(x, wg, wu, wd):
    return (jax.nn.silu(x @ wg) * (x @ wu)) @ wd


def route(xf, w_router, router_bias):
    t = xf.shape[0]
    scores = jax.nn.sigmoid(xf.astype(jnp.float32) @ w_router.astype(jnp.float32))
    sel = scores + router_bias.astype(jnp.float32)
    grp = sel.reshape(t, N_GROUPS, N_EXPERTS // N_GROUPS)
    group_score = lax.top_k(grp, 2)[0].sum(-1)
    _, gidx = lax.top_k(group_score, TOPK_GROUPS)
    gmask = jax.nn.one_hot(gidx, N_GROUPS, dtype=jnp.float32).sum(-2) > 0
    emask = jnp.repeat(gmask, N_EXPERTS // N_GROUPS, axis=-1)
    _, eidx = lax.top_k(jnp.where(emask, sel, -jnp.inf), TOP_K)
    w = jnp.take_along_axis(scores, eidx, axis=-1)
    w = w / jnp.sum(w, axis=-1, keepdims=True) * ROUTED_SCALE
    return eidx.astype(jnp.int32), w


def moe_ffn(u, w_router, router_bias, w_gate, w_up, w_down, ws_gate, ws_up, ws_down):
    bsz, s, d = u.shape
    t = bsz * s
    xf = u.reshape(t, d)
    eidx, w = route(xf, w_router, router_bias)
    n_assign = t * TOP_K
    flat_e = eidx.reshape(n_assign)
    flat_tok = jnp.repeat(jnp.arange(t, dtype=jnp.int32), TOP_K)
    flat_w = w.reshape(n_assign)
    order = jnp.argsort(flat_e)
    se = flat_e[order]
    counts = jnp.zeros((N_EXPERTS,), jnp.int32).at[flat_e].add(1)
    blocks_per_e = (counts + EXPERT_BLOCK - 1) // EXPERT_BLOCK
    blk_end = jnp.cumsum(blocks_per_e)
    pad_start = (blk_end - blocks_per_e) * EXPERT_BLOCK
    start = jnp.cumsum(counts) - counts
    dest = pad_start[se] + jnp.arange(n_assign, dtype=jnp.int32) - start[se]
    n_blocks = -(-n_assign // EXPERT_BLOCK) + N_EXPERTS
    n_rows = n_blocks * EXPERT_BLOCK
    tok_buf = jnp.zeros((n_rows,), jnp.int32).at[dest].set(flat_tok[order])
    w_buf = jnp.zeros((n_rows,), flat_w.dtype).at[dest].set(flat_w[order])
    blk_expert = jnp.minimum(
        jnp.searchsorted(blk_end, jnp.arange(n_blocks, dtype=jnp.int32), side='right'),
        N_EXPERTS - 1).astype(jnp.int32)

    def expert_block(args):
        tok, e = args
        return swiglu(xf[tok], w_gate[e], w_up[e], w_down[e])

    out = lax.map(expert_block, (tok_buf.reshape(n_blocks, EXPERT_BLOCK), blk_expert))
    out = out.reshape(n_rows, d) * w_buf[:, None].astype(out.dtype)
    routed = jax.ops.segment_sum(out, tok_buf, num_segments=t)
    shared = swiglu(xf, ws_gate, ws_up, ws_down)
    return (routed + shared).reshape(bsz, s, d)


def setup_inputs(seed: int = 0) -> dict:
    key = jax.random.key(seed)
    ks = jax.random.split(key, 24)

    def nrm(k, shape, scale):
        return jax.random.normal(k, shape, jnp.float32) * scale

    d_in = 3 * D_CONV + 3 * D_ATT
    return {
        "x": nrm(ks[0], (BATCH, SEQ, D_MODEL), 1.0),
        "c": nrm(ks[1], (BATCH, D_MODEL), 1.0),
        "w_mod": nrm(ks[2], (DEPTH, D_MODEL, N_MOD * D_MODEL), 0.1 * D_MODEL ** -0.5),
        "b_mod": nrm(ks[3], (DEPTH, N_MOD * D_MODEL), 0.01),
        "w_in": nrm(ks[4], (DEPTH, D_MODEL, d_in), D_MODEL ** -0.5),
        "conv_w": nrm(ks[5], (DEPTH, CONV_WIDTH, D_CONV), CONV_WIDTH ** -0.5),
        "g_conv": 1.0 + nrm(ks[6], (DEPTH, D_CONV), 0.02),
        "g_att": 1.0 + nrm(ks[7], (DEPTH, D_ATT), 0.02),
        "w_out": nrm(ks[8], (DEPTH, D_MIX, D_MODEL), DEEPNORM_BETA * D_MIX ** -0.5),
        "ln1_g": 1.0 + nrm(ks[9], (DEPTH, D_MODEL), 0.02),
        "ln1_b": nrm(ks[10], (DEPTH, D_MODEL), 0.01),
        "w_router": nrm(ks[11], (DEPTH, D_MODEL, N_EXPERTS), D_MODEL ** -0.5),
        "router_bias": nrm(ks[12], (DEPTH, N_EXPERTS), 0.01),
        "w_gate": nrm(ks[13], (DEPTH, N_EXPERTS, D_MODEL, D_EXPERT), D_MODEL ** -0.5),
        "w_up": nrm(ks[14], (DEPTH, N_EXPERTS, D_MODEL, D_EXPERT), DEEPNORM_BETA * D_MODEL ** -0.5),
        "w_down": nrm(ks[15], (DEPTH, N_EXPERTS, D_EXPERT, D_MODEL), DEEPNORM_BETA * D_EXPERT ** -0.5),
        "ws_gate": nrm(ks[16], (DEPTH, D_MODEL, D_EXPERT), D_MODEL ** -0.5),
        "ws_up": nrm(ks[17], (DEPTH, D_MODEL, D_EXPERT), DEEPNORM_BETA * D_MODEL ** -0.5),
        "ws_down": nrm(ks[18], (DEPTH, D_EXPERT, D_MODEL), DEEPNORM_BETA * D_EXPERT ** -0.5),
        "ln2_g": 1.0 + nrm(ks[19], (DEPTH, D_MODEL), 0.02),
        "ln2_b": nrm(ks[20], (DEPTH, D_MODEL), 0.01),
    }


def reference(x, c, w_mod, b_mod, w_in, conv_w, g_conv, g_att, w_out, ln1_g, ln1_b,
              w_router, router_bias, w_gate, w_up, w_down, ws_gate, ws_up, ws_down,
              ln2_g, ln2_b):
    for l in range(DEPTH):
        mod = (c @ w_mod[l] + b_mod[l])[:, None, :]
        sh1, sc1, g1, sh2, sc2, g2 = jnp.split(mod, N_MOD, axis=-1)
        u = x * (1.0 + sc1) + sh1
        mix = hybrid_mixer(u, w_in[l], conv_w[l], g_conv[l], g_att[l], w_out[l])
        x = layer_norm(DEEPNORM_ALPHA * x + (1.0 + g1) * mix, ln1_g[l], ln1_b[l])
        u = x * (1.0 + sc2) + sh2
        ffn = moe_ffn(u, w_router[l], router_bias[l], w_gate[l], w_up[l], w_down[l],
                      ws_gate[l], ws_up[l], ws_down[l])
        x = layer_norm(DEEPNORM_ALPHA * x + (1.0 + g2) * ffn, ln2_g[l], ln2_b[l])
    return x
```

```python
import functools

import jax
import jax.numpy as jnp
from jax import lax
from jax.experimental import pallas as pl
from jax.experimental.pallas import tpu as pltpu

F32 = jnp.float32
BF16 = jnp.bfloat16
I32 = jnp.int32

D_MODEL = 2048
DEPTH = 4
D_CONV = D_MODEL // 2
D_ATT = D_MODEL // 2
HEAD_DIM = 128
N_HEADS = D_ATT // HEAD_DIM
N_EXPERTS = 64
TOP_K = 8
N_GROUPS = 8
GROUP_SIZE = N_EXPERTS // N_GROUPS
TOPK_GROUPS = 4
D_EXPERT = 384
ROUTED_SCALE = 2.5
EXPERT_BLOCK = 256
DEEPNORM_ALPHA = (2 * DEPTH) ** 0.25
LN_EPS = 1e-5
RMS_EPS = 1e-6
N_MOD = 6
ATT_SCALE = HEAD_DIM ** -0.5

LANES = 128
SUBLANES = 8
VMEM_LIMIT = 48 * 1024 * 1024

MOD_TN = 1024
INPROJ_TM = 512
INPROJ_TN = 1024
CONV_TM = 512
ATT_BQ = 256
ATT_BK = 256
OUT_TM = 256
ROUTER_TM = 512
COMBINE_TM = 128


def _params(*sem):
    return pltpu.CompilerParams(dimension_semantics=sem, vmem_limit_bytes=VMEM_LIMIT)


def _mod_kernel(c_ref, w_ref, b_ref, o_ref):
    o_ref[...] = jnp.dot(c_ref[...], w_ref[...], preferred_element_type=F32) + b_ref[...]


def _modulation(c, w_mod, b_mod):
    n = N_MOD * D_MODEL
    c8 = jnp.broadcast_to(c, (SUBLANES, D_MODEL))
    out = pl.pallas_call(
        _mod_kernel,
        out_shape=jax.ShapeDtypeStruct((DEPTH, SUBLANES, n), F32),
        grid=(DEPTH, n // MOD_TN),
        in_specs=[
            pl.BlockSpec((SUBLANES, D_MODEL), lambda l, j: (0, 0)),
            pl.BlockSpec((None, D_MODEL, MOD_TN), lambda l, j: (l, 0, j)),
            pl.BlockSpec((None, 1, MOD_TN), lambda l, j: (l, 0, j)),
        ],
        out_specs=pl.BlockSpec((None, SUBLANES, MOD_TN), lambda l, j: (l, 0, j)),
        compiler_params=_params("arbitrary", "arbitrary"),
        name="modulation",
    )(c8, w_mod, b_mod.reshape(DEPTH, 1, n))
    return out[:, 0:1, :]


def _inproj_kernel(x_ref, sc_ref, sh_ref, w_ref, o_ref, *, q_tiles):
    u = (x_ref[...] * (1.0 + sc_ref[...]) + sh_ref[...]).astype(BF16)
    acc = jnp.dot(u, w_ref[...], preferred_element_type=F32)
    if q_tiles:
        acc = acc * jnp.where(pl.program_id(0) < q_tiles, ATT_SCALE, 1.0)
    o_ref[...] = acc.astype(o_ref.dtype)


def _inproj(x, sc, sh, w_bf, layer, col0, out_dtype, q_tiles):
    t = x.shape[0]
    n = 3 * D_CONV
    tile0 = col0 // INPROJ_TN
    return pl.pallas_call(
        functools.partial(_inproj_kernel, q_tiles=q_tiles),
        out_shape=jax.ShapeDtypeStruct((t, n), out_dtype),
        grid=(n // INPROJ_TN, t // INPROJ_TM),
        in_specs=[
            pl.BlockSpec((INPROJ_TM, D_MODEL), lambda j, i: (i, 0)),
            pl.BlockSpec((1, D_MODEL), lambda j, i: (0, 0)),
            pl.BlockSpec((1, D_MODEL), lambda j, i: (0, 0)),
            pl.BlockSpec((None, D_MODEL, INPROJ_TN), lambda j, i: (layer, 0, tile0 + j)),
        ],
        out_specs=pl.BlockSpec((INPROJ_TM, INPROJ_TN), lambda j, i: (i, j)),
        compiler_params=_params("arbitrary", "arbitrary"),
        name="inproj",
    )(x, sc, sh, w_bf)


def _conv_kernel(h_ref, b_ref, c_ref, hh_ref, ch_ref, cw_ref, g_ref, o_ref):
    i = pl.program_id(0)
    tm = h_ref.shape[0]
    v = c_ref[...] * h_ref[...]
    halo = jnp.where(i > 0, ch_ref[...] * hh_ref[...], 0.0)
    row = lax.broadcasted_iota(I32, v.shape, 0)
    prev1 = halo[SUBLANES - 1:SUBLANES, :]
    prev2 = halo[SUBLANES - 2:SUBLANES - 1, :]
    v1 = jnp.where(row == 0, prev1, pltpu.roll(v, 1, axis=0))
    v2 = jnp.where(row == 0, prev2, jnp.where(row == 1, prev1, pltpu.roll(v, 2, axis=0)))
    cw = cw_ref[...]
    conv = cw[0:1, :] * v2 + cw[1:2, :] * v1 + cw[2:3, :] * v
    y = b_ref[...] * conv
    ms = jnp.mean(y * y, axis=-1, keepdims=True)
    o_ref[...] = (y * lax.rsqrt(ms + RMS_EPS) * g_ref[...]).astype(o_ref.dtype)
    del tm


def _conv(hbc, conv_w, g_conv, layer):
    t = hbc.shape[0]
    halo_blocks = CONV_TM // SUBLANES

    def halo_map(col):
        return lambda i: (jnp.maximum(i * halo_blocks - 1, 0), col)

    return pl.pallas_call(
        _conv_kernel,
        out_shape=jax.ShapeDtypeStruct((t, D_CONV), BF16),
        grid=(t // CONV_TM,),
        in_specs=[
            pl.BlockSpec((CONV_TM, D_CONV), lambda i: (i, 0)),
            pl.BlockSpec((CONV_TM, D_CONV), lambda i: (i, 1)),
            pl.BlockSpec((CONV_TM, D_CONV), lambda i: (i, 2)),
            pl.BlockSpec((SUBLANES, D_CONV), halo_map(0)),
            pl.BlockSpec((SUBLANES, D_CONV), halo_map(2)),
            pl.BlockSpec((None, 3, D_CONV), lambda i: (layer, 0, 0)),
            pl.BlockSpec((None, 1, D_CONV), lambda i: (layer, 0, 0)),
        ],
        out_specs=pl.BlockSpec((CONV_TM, D_CONV), lambda i: (i, 0)),
        compiler_params=_params("arbitrary"),
        name="conv_rms",
    )(hbc, hbc, hbc, hbc, hbc, conv_w, g_conv.reshape(DEPTH, 1, D_CONV))


def _attn_kernel(q_ref, k_ref, v_ref, o_ref, *, bq, bk):
    i = pl.program_id(1)
    q = q_ref[...]
    row = lax.broadcasted_iota(I32, (bq, bk), 0)
    col = lax.broadcasted_iota(I32, (bq, bk), 1)
    diff = col - row
    r2 = lax.broadcasted_iota(I32, (bk, bk + LANES), 0)
    c2 = lax.broadcasted_iota(I32, (bk, bk + LANES), 1)
    tri = jnp.where((r2 >= c2) | (c2 >= bk), 1.0, 0.0).astype(BF16)

    def body(jj, carry):
        acc, later = carry
        j = i - jj
        start = pl.multiple_of(j * bk, bk)
        ks = k_ref[pl.ds(start, bk), :]
        vs = v_ref[pl.ds(start, bk), :]
        z = lax.dot_general(q, ks, (((1,), (1,)), ((), ())), preferred_element_type=F32)
        mask = diff < jj * bq
        softplus = jnp.maximum(z, 0.0) + jnp.log(1.0 + jnp.exp(-jnp.abs(z)))
        log_keep = jnp.where(mask, -softplus, 0.0)
        hi = log_keep.astype(BF16)
        lo = (log_keep - hi.astype(F32)).astype(BF16)
        sums = (jnp.dot(hi, tri, preferred_element_type=F32)
                + jnp.dot(lo, tri, preferred_element_type=F32))
        incl = sums[:, :bk] + jnp.concatenate([later] * (bk // LANES), axis=1)
        a = jnp.where(mask, jnp.exp(z + incl), 0.0)
        acc = acc + jnp.dot(a.astype(BF16), vs, preferred_element_type=F32)
        later = later + sums[:, bk:]
        return acc, later

    zeros = jnp.zeros((bq, LANES), F32)
    acc, _ = lax.fori_loop(0, i + 1, body, (zeros, zeros))
    o_ref[...] = acc


def _attention(qkv):
    t = qkv.shape[0]
    bq, bk = ATT_BQ, ATT_BK
    assert bq == bk and HEAD_DIM == LANES
    return pl.pallas_call(
        functools.partial(_attn_kernel, bq=bq, bk=bk),
        out_shape=jax.ShapeDtypeStruct((t, D_ATT), F32),
        grid=(N_HEADS, t // bq),
        in_specs=[
            pl.BlockSpec((bq, HEAD_DIM), lambda h, i: (i, h)),
            pl.BlockSpec((t, HEAD_DIM), lambda h, i: (0, N_HEADS + h)),
            pl.BlockSpec((t, HEAD_DIM), lambda h, i: (0, 2 * N_HEADS + h)),
        ],
        out_specs=pl.BlockSpec((bq, HEAD_DIM), lambda h, i: (i, h)),
        compiler_params=_params("arbitrary", "arbitrary"),
        name="stickbreak_attn",
    )(qkv, qkv, qkv)


def _layer_norm(r, g, b):
    mu = jnp.mean(r, axis=-1, keepdims=True)
    rc = r - mu
    var = jnp.mean(rc * rc, axis=-1, keepdims=True)
    return rc * lax.rsqrt(var + LN_EPS) * g + b


def _outproj_kernel(yc_ref, att_ref, ga_ref, wc_ref, wa_ref, x_ref, g1_ref, lg_ref, lb_ref, o_ref):
    att = att_ref[...]
    ms = jnp.mean(att * att, axis=-1, keepdims=True)
    ya = (att * lax.rsqrt(ms + RMS_EPS) * ga_ref[...]).astype(BF16)
    mix = (jnp.dot(yc_ref[...], wc_ref[...], preferred_element_type=F32)
           + jnp.dot(ya, wa_ref[...], preferred_element_type=F32))
    r = DEEPNORM_ALPHA * x_ref[...] + (1.0 + g1_ref[...]) * mix
    o_ref[...] = _layer_norm(r, lg_ref[...], lb_ref[...])


def _outproj(yc, att, g_att, w_out_bf, x, g1, ln_g, ln_b, layer):
    t = x.shape[0]
    vec = lambda n: pl.BlockSpec((None, 1, n), lambda i: (layer, 0, 0))
    return pl.pallas_call(
        _outproj_kernel,
        out_shape=jax.ShapeDtypeStruct((t, D_MODEL), F32),
        grid=(t // OUT_TM,),
        in_specs=[
            pl.BlockSpec((OUT_TM, D_CONV), lambda i: (i, 0)),
            pl.BlockSpec((OUT_TM, D_ATT), lambda i: (i, 0)),
            vec(D_ATT),
            pl.BlockSpec((None, D_CONV, D_MODEL), lambda i: (layer, 0, 0)),
            pl.BlockSpec((None, D_ATT, D_MODEL), lambda i: (layer, 1, 0)),
            pl.BlockSpec((OUT_TM, D_MODEL), lambda i: (i, 0)),
            pl.BlockSpec((1, D_MODEL), lambda i: (0, 0)),
            vec(D_MODEL),
            vec(D_MODEL),
        ],
        out_specs=pl.BlockSpec((OUT_TM, D_MODEL), lambda i: (i, 0)),
        compiler_params=_params("arbitrary"),
        name="outproj_ln",
    )(yc, att, g_att.reshape(DEPTH, 1, D_ATT), w_out_bf, w_out_bf, x, g1,
      ln_g.reshape(DEPTH, 1, D_MODEL), ln_b.reshape(DEPTH, 1, D_MODEL))


def _split_bf16(a):
    hi = a.astype(BF16)
    lo = (a - hi.astype(F32)).astype(BF16)
    return hi, lo


def _router_kernel(x_ref, sc_ref, sh_ref, wr_ref, bias_ref,
                   u_ref, eidx_ref, wts_ref, rank_ref, cnt_ref):
    step = pl.program_id(0)
    tm = x_ref.shape[0]
    neg_inf = -jnp.inf

    @pl.when(step == 0)
    def _():
        cnt_ref[...] = jnp.zeros_like(cnt_ref)

    u = x_ref[...] * (1.0 + sc_ref[...]) + sh_ref[...]
    u_ref[...] = u
    u_hi, u_lo = _split_bf16(u)
    w_hi, w_lo = _split_bf16(wr_ref[...])
    nt = (((1,), (1,)), ((), ()))
    logits = (lax.dot_general(w_hi, u_hi, nt, preferred_element_type=F32)
              + lax.dot_general(w_hi, u_lo, nt, preferred_element_type=F32)
              + lax.dot_general(w_lo, u_hi, nt, preferred_element_type=F32))
    scores = 1.0 / (1.0 + jnp.exp(-logits))
    sel = scores + bias_ref[...]

    gscore = []
    for g in range(N_GROUPS):
        grp = sel[g * GROUP_SIZE:(g + 1) * GROUP_SIZE, :]
        m1 = jnp.max(grp, axis=0, keepdims=True)
        is_max = grp == m1
        n_max = jnp.sum(is_max.astype(F32), axis=0, keepdims=True)
        rest = jnp.max(jnp.where(is_max, neg_inf, grp), axis=0, keepdims=True)
        gscore.append(m1 + jnp.where(n_max >= 2.0, m1, rest))
    masked = []
    for g in range(N_GROUPS):
        beaten = jnp.zeros((1, tm), F32)
        for o in range(N_GROUPS):
            if o == g:
                continue
            wins = (gscore[o] >= gscore[g]) if o < g else (gscore[o] > gscore[g])
            beaten = beaten + wins.astype(F32)
        keep = beaten < float(TOPK_GROUPS)
        grp = sel[g * GROUP_SIZE:(g + 1) * GROUP_SIZE, :]
        masked.append(jnp.where(keep, grp, neg_inf))
    cand = jnp.concatenate(masked, axis=0)

    eiota = lax.broadcasted_iota(I32, (N_EXPERTS, tm), 0)
    picked = jnp.zeros((N_EXPERTS, tm), F32)
    idxs, ws = [], []
    for _ in range(TOP_K):
        m = jnp.max(cand, axis=0, keepdims=True)
        idx = jnp.min(jnp.where(cand == m, eiota, N_EXPERTS), axis=0, keepdims=True)
        hit = eiota == idx
        ws.append(jnp.sum(jnp.where(hit, scores, 0.0), axis=0, keepdims=True))
        idxs.append(idx)
        picked = jnp.where(hit, 1.0, picked)
        cand = jnp.where(hit, neg_inf, cand)
    wsum = ws[0]
    for k in range(1, TOP_K):
        wsum = wsum + ws[k]
    norm = ROUTED_SCALE / wsum

    r = lax.broadcasted_iota(I32, (tm, tm), 0)
    c = lax.broadcasted_iota(I32, (tm, tm), 1)
    before = jnp.where(r < c, 1.0, 0.0).astype(BF16)
    ones = jnp.ones((tm, LANES), BF16)
    picked_bf = picked.astype(BF16)
    prior = jnp.dot(picked_bf, before, preferred_element_type=F32)
    total = jnp.dot(picked_bf, ones, preferred_element_type=F32)
    carried = cnt_ref[...]
    rank_full = prior + jnp.concatenate([carried] * (tm // LANES), axis=1)
    cnt_ref[...] = carried + total

    for k in range(TOP_K):
        hit = eiota == idxs[k]
        eidx_ref[k:k + 1, :] = idxs[k]
        wts_ref[k:k + 1, :] = ws[k] * norm
        rank_ref[k:k + 1, :] = jnp.sum(jnp.where(hit, rank_full, 0.0), axis=0,
                                       keepdims=True).astype(I32)


def _router(x, sc, sh, w_router_t, router_bias, layer):
    t = x.shape[0]
    tm = ROUTER_TM
    return pl.pallas_call(
        _router_kernel,
        out_shape=(
            jax.ShapeDtypeStruct((t, D_MODEL), F32),
            jax.ShapeDtypeStruct((TOP_K, t), I32),
            jax.ShapeDtypeStruct((TOP_K, t), F32),
            jax.ShapeDtypeStruct((TOP_K, t), I32),
            jax.ShapeDtypeStruct((N_EXPERTS, LANES), F32),
        ),
        grid=(t // tm,),
        in_specs=[
            pl.BlockSpec((tm, D_MODEL), lambda i: (i, 0)),
            pl.BlockSpec((1, D_MODEL), lambda i: (0, 0)),
            pl.BlockSpec((1, D_MODEL), lambda i: (0, 0)),
            pl.BlockSpec((None, N_EXPERTS, D_MODEL), lambda i: (layer, 0, 0)),
            pl.BlockSpec((None, N_EXPERTS, 1), lambda i: (layer, 0, 0)),
        ],
        out_specs=(
            pl.BlockSpec((tm, D_MODEL), lambda i: (i, 0)),
            pl.BlockSpec((TOP_K, tm), lambda i: (0, i)),
            pl.BlockSpec((TOP_K, tm), lambda i: (0, i)),
            pl.BlockSpec((TOP_K, tm), lambda i: (0, i)),
            pl.BlockSpec((N_EXPERTS, LANES), lambda i: (0, 0)),
        ),
        compiler_params=_params("arbitrary"),
        name="router",
    )(x, sc, sh, w_router_t, router_bias.reshape(DEPTH, N_EXPERTS, 1))


def _row_copy(src_hbm, row, dst, slot, sem):
    return pltpu.make_async_copy(src_hbm.at[pl.ds(row, 1), :], dst.at[pl.ds(slot, 1), :], sem)


def _expert_kernel(tok_ref, be_ref, nused_ref, u_hbm, wg_ref, wu_ref, wd_ref, o_ref, xbuf, sem):
    b = pl.program_id(0)
    rows = xbuf.shape[0]
    base = b * rows

    @pl.when(b < nused_ref[0])
    def _():
        def issue(r, _):
            _row_copy(u_hbm, tok_ref[base + r], xbuf, r, sem).start()
            return 0

        lax.fori_loop(0, rows, issue, 0, unroll=8)

        def drain(r, _):
            _row_copy(u_hbm, tok_ref[base + r], xbuf, r, sem).wait()
            return 0

        lax.fori_loop(0, rows, drain, 0, unroll=8)
        x = xbuf[...].astype(BF16)
        g = jnp.dot(x, wg_ref[...], preferred_element_type=F32)
        up = jnp.dot(x, wu_ref[...], preferred_element_type=F32)
        h = (g * (1.0 / (1.0 + jnp.exp(-g))) * up).astype(BF16)
        o_ref[...] = jnp.dot(h, wd_ref[...], preferred_element_type=F32)

    @pl.when(b >= nused_ref[0])
    def _():
        o_ref[...] = jnp.zeros_like(o_ref)


def _experts(tok_buf, blk_expert, n_used, u, wg_bf, wu_bf, wd_bf, layer, n_blocks):
    n_rows = n_blocks * EXPERT_BLOCK
    grid_spec = pltpu.PrefetchScalarGridSpec(
        num_scalar_prefetch=3,
        grid=(n_blocks,),
        in_specs=[
            pl.BlockSpec(memory_space=pl.ANY),
            pl.BlockSpec((None, None, D_MODEL, D_EXPERT), lambda b, tok, be, nu: (layer, be[b], 0, 0)),
            pl.BlockSpec((None, None, D_MODEL, D_EXPERT), lambda b, tok, be, nu: (layer, be[b], 0, 0)),
            pl.BlockSpec((None, None, D_EXPERT, D_MODEL), lambda b, tok, be, nu: (layer, be[b], 0, 0)),
        ],
        out_specs=pl.BlockSpec((EXPERT_BLOCK, D_MODEL), lambda b, tok, be, nu: (b, 0)),
        scratch_shapes=[
            pltpu.VMEM((EXPERT_BLOCK, D_MODEL), F32),
            pltpu.SemaphoreType.DMA(()),
        ],
    )
    return pl.pallas_call(
        _expert_kernel,
        out_shape=jax.ShapeDtypeStruct((n_rows, D_MODEL), F32),
        grid_spec=grid_spec,
        compiler_params=_params("arbitrary"),
        name="experts",
    )(tok_buf, blk_expert, n_used, u, wg_bf, wu_bf, wd_bf)


def _combine_kernel(dest_ref, eo_hbm, wts_ref, x_ref, sc_ref, sh_ref, g2_ref,
                    sg_ref, su_ref, sd_ref, lg_ref, lb_ref, o_ref, gbuf, sem, *, n_tok):
    i = pl.program_id(0)
    tm = x_ref.shape[0]

    for k in range(TOP_K):
        base = k * n_tok + i * tm

        def issue(r, _, k=k, base=base):
            _row_copy(eo_hbm, dest_ref[base + r], gbuf.at[k], r, sem).start()
            return 0

        lax.fori_loop(0, tm, issue, 0, unroll=8)

    x = x_ref[...]
    u = (x * (1.0 + sc_ref[...]) + sh_ref[...]).astype(BF16)
    g = jnp.dot(u, sg_ref[...], preferred_element_type=F32)
    up = jnp.dot(u, su_ref[...], preferred_element_type=F32)
    h = (g * (1.0 / (1.0 + jnp.exp(-g))) * up).astype(BF16)
    ffn = jnp.dot(h, sd_ref[...], preferred_element_type=F32)

    for k in range(TOP_K):
        base = k * n_tok + i * tm

        def drain(r, _, k=k, base=base):
            _row_copy(eo_hbm, dest_ref[base + r], gbuf.at[k], r, sem).wait()
            return 0

        lax.fori_loop(0, tm, drain, 0, unroll=8)

    wts = wts_ref[...]
    for k in range(TOP_K):
        ffn = ffn + wts[:, k:k + 1] * gbuf[k]
    r = DEEPNORM_ALPHA * x + (1.0 + g2_ref[...]) * ffn
    o_ref[...] = _layer_norm(r, lg_ref[...], lb_ref[...])


def _combine(dest_flat, eo, wts_tk, x, sc, sh, g2, sg_bf, su_bf, sd_bf, ln_g, ln_b, layer):
    t = x.shape[0]
    tm = COMBINE_TM
    vec = lambda n: pl.BlockSpec((None, 1, n), lambda i, d: (layer, 0, 0))
    row = lambda: pl.BlockSpec((1, D_MODEL), lambda i, d: (0, 0))
    grid_spec = pltpu.PrefetchScalarGridSpec(
        num_scalar_prefetch=1,
        grid=(t // tm,),
        in_specs=[
            pl.BlockSpec(memory_space=pl.ANY),
            pl.BlockSpec((tm, TOP_K), lambda i, d: (i, 0)),
            pl.BlockSpec((tm, D_MODEL), lambda i, d: (i, 0)),
            row(), row(), row(),
            pl.BlockSpec((None, D_MODEL, D_EXPERT), lambda i, d: (layer, 0, 0)),
            pl.BlockSpec((None, D_MODEL, D_EXPERT), lambda i, d: (layer, 0, 0)),
            pl.BlockSpec((None, D_EXPERT, D_MODEL), lambda i, d: (layer, 0, 0)),
            vec(D_MODEL), vec(D_MODEL),
        ],
        out_specs=pl.BlockSpec((tm, D_MODEL), lambda i, d: (i, 0)),
        scratch_shapes=[
            pltpu.VMEM((TOP_K, tm, D_MODEL), F32),
            pltpu.SemaphoreType.DMA(()),
        ],
    )
    return pl.pallas_call(
        functools.partial(_combine_kernel, n_tok=t),
        out_shape=jax.ShapeDtypeStruct((t, D_MODEL), F32),
        grid_spec=grid_spec,
        compiler_params=_params("arbitrary"),
        name="combine_ln",
    )(dest_flat, eo, wts_tk, x, sc, sh, g2, sg_bf, su_bf, sd_bf,
      ln_g.reshape(DEPTH, 1, D_MODEL), ln_b.reshape(DEPTH, 1, D_MODEL))


def _dispatch_plan(eidx, rank, counts, n_tok, n_blocks):
    blocks_per_e = (counts + EXPERT_BLOCK - 1) // EXPERT_BLOCK
    blk_end = jnp.cumsum(blocks_per_e)
    pad_start = (blk_end - blocks_per_e) * EXPERT_BLOCK
    dest = pad_start[eidx] + rank
    tok = jnp.broadcast_to(jnp.arange(n_tok, dtype=I32)[None, :], (TOP_K, n_tok))
    tok_buf = jnp.zeros((n_blocks * EXPERT_BLOCK,), I32).at[dest.reshape(-1)].set(tok.reshape(-1))
    blk_expert = jnp.minimum(
        jnp.searchsorted(blk_end, jnp.arange(n_blocks, dtype=I32), side="right"),
        N_EXPERTS - 1).astype(I32)
    n_used = blk_end[-1:].astype(I32)
    return dest.reshape(-1).astype(I32), tok_buf, blk_expert, n_used


def kernel(x, c, w_mod, b_mod, w_in, conv_w, g_conv, g_att, w_out, ln1_g, ln1_b, w_router,
           router_bias, w_gate, w_up, w_down, ws_gate, ws_up, ws_down, ln2_g, ln2_b):
    bsz, seq, d = x.shape
    assert bsz == 1 and d == D_MODEL
    t = bsz * seq
    n_blocks = -(-(t * TOP_K) // EXPERT_BLOCK) + N_EXPERTS
    xt = x.reshape(t, d)

    mod = _modulation(c, w_mod, b_mod)
    w_in_bf = w_in.astype(BF16)
    w_out_bf = w_out.astype(BF16)
    wg_bf, wu_bf, wd_bf = w_gate.astype(BF16), w_up.astype(BF16), w_down.astype(BF16)
    sg_bf, su_bf, sd_bf = ws_gate.astype(BF16), ws_up.astype(BF16), ws_down.astype(BF16)
    w_router_t = jnp.swapaxes(w_router, 1, 2)

    for l in range(DEPTH):
        sh1, sc1, g1, sh2, sc2, g2 = [mod[l, :, n * d:(n + 1) * d] for n in range(N_MOD)]
        hbc = _inproj(xt, sc1, sh1, w_in_bf, l, 0, F32, 0)
        qkv = _inproj(xt, sc1, sh1, w_in_bf, l, 3 * D_CONV, BF16, D_ATT // INPROJ_TN)
        yc = _conv(hbc, conv_w, g_conv, l)
        att = _attention(qkv)
        xt = _outproj(yc, att, g_att, w_out_bf, xt, g1, ln1_g, ln1_b, l)
        u, eidx, wts, rank, cnt = _router(xt, sc2, sh2, w_router_t, router_bias, l)
        counts = cnt[:, 0].astype(I32)
        dest, tok_buf, blk_expert, n_used = _dispatch_plan(eidx, rank, counts, t, n_blocks)
        eo = _experts(tok_buf, blk_expert, n_used, u, wg_bf, wu_bf, wd_bf, l, n_blocks)
        xt = _combine(dest, eo, wts.T, xt, sc2, sh2, g2, sg_bf, su_bf, sd_bf, ln2_g, ln2_b, l)
    return xt.reshape(bsz, seq, d)
```

```python
import functools

import jax
import jax.numpy as jnp
from jax import lax
from jax.experimental import pallas as pl
from jax.experimental.pallas import tpu as pltpu

F32 = jnp.float32
BF16 = jnp.bfloat16
I32 = jnp.int32
U32 = jnp.uint32

D_MODEL = 2048
DEPTH = 4
D_CONV = D_MODEL // 2
D_ATT = D_MODEL // 2
HEAD_DIM = 128
N_HEADS = D_ATT // HEAD_DIM
N_EXPERTS = 64
TOP_K = 8
N_GROUPS = 8
GROUP_SIZE = N_EXPERTS // N_GROUPS
TOPK_GROUPS = 4
D_EXPERT = 384
ROUTED_SCALE = 2.5
EXPERT_BLOCK = 256
DEEPNORM_ALPHA = (2 * DEPTH) ** 0.25
LN_EPS = 1e-5
RMS_EPS = 1e-6
N_MOD = 6
ATT_SCALE = HEAD_DIM ** -0.5
EXP_UNDERFLOW = -110.0

LANES = 128
SUBLANES = 8
VMEM_LIMIT = 48 * 1024 * 1024

MOD_TN = 1024
INPROJ_TM = 512
INPROJ_TN = 1024
CONV_TM = 512
ATT_BQ = 256
ATT_BK = 256
OUT_TM = 256
ROUTER_TM = 512
COMBINE_TM = 128
DISPATCH_TM = 256
DMA_UNROLL = 8


def _params(*sem):
    return pltpu.CompilerParams(dimension_semantics=sem, vmem_limit_bytes=VMEM_LIMIT)


def _mod_kernel(c_ref, w_ref, b_ref, o_ref):
    o_ref[...] = jnp.dot(c_ref[...], w_ref[...], preferred_element_type=F32) + b_ref[...]


def _modulation(c, w_mod, b_mod):
    n = N_MOD * D_MODEL
    c8 = jnp.broadcast_to(c, (SUBLANES, D_MODEL))
    out = pl.pallas_call(
        _mod_kernel,
        out_shape=jax.ShapeDtypeStruct((DEPTH, SUBLANES, n), F32),
        grid=(DEPTH, n // MOD_TN),
        in_specs=[
            pl.BlockSpec((SUBLANES, D_MODEL), lambda l, j: (0, 0)),
            pl.BlockSpec((None, D_MODEL, MOD_TN), lambda l, j: (l, 0, j)),
            pl.BlockSpec((None, 1, MOD_TN), lambda l, j: (l, 0, j)),
        ],
        out_specs=pl.BlockSpec((None, SUBLANES, MOD_TN), lambda l, j: (l, 0, j)),
        compiler_params=_params("arbitrary", "arbitrary"),
        name="modulation",
    )(c8, w_mod, b_mod.reshape(DEPTH, 1, n))
    return out[:, 0:1, :]


def _inproj_kernel(x_ref, sc_ref, sh_ref, w_ref, o_ref, *, q_tiles):
    u = (x_ref[...] * (1.0 + sc_ref[...]) + sh_ref[...]).astype(BF16)
    acc = jnp.dot(u, w_ref[...], preferred_element_type=F32)
    if q_tiles:
        acc = acc * jnp.where(pl.program_id(0) < q_tiles, ATT_SCALE, 1.0)
    o_ref[...] = acc.astype(o_ref.dtype)


def _inproj(x, sc, sh, w_bf, layer, col0, out_dtype, q_tiles):
    t = x.shape[0]
    n = 3 * D_CONV
    tile0 = col0 // INPROJ_TN
    return pl.pallas_call(
        functools.partial(_inproj_kernel, q_tiles=q_tiles),
        out_shape=jax.ShapeDtypeStruct((t, n), out_dtype),
        grid=(n // INPROJ_TN, t // INPROJ_TM),
        in_specs=[
            pl.BlockSpec((INPROJ_TM, D_MODEL), lambda j, i: (i, 0)),
            pl.BlockSpec((1, D_MODEL), lambda j, i: (0, 0)),
            pl.BlockSpec((1, D_MODEL), lambda j, i: (0, 0)),
            pl.BlockSpec((None, D_MODEL, INPROJ_TN), lambda j, i: (layer, 0, tile0 + j)),
        ],
        out_specs=pl.BlockSpec((INPROJ_TM, INPROJ_TN), lambda j, i: (i, j)),
        compiler_params=_params("arbitrary", "arbitrary"),
        name="inproj",
    )(x, sc, sh, w_bf)


def _conv_kernel(h_ref, b_ref, c_ref, hh_ref, ch_ref, cw_ref, g_ref, o_ref):
    i = pl.program_id(0)
    v = c_ref[...] * h_ref[...]
    halo = jnp.where(i > 0, ch_ref[...] * hh_ref[...], 0.0)
    row = lax.broadcasted_iota(I32, v.shape, 0)
    prev1 = halo[SUBLANES - 1:SUBLANES, :]
    prev2 = halo[SUBLANES - 2:SUBLANES - 1, :]
    v1 = jnp.where(row == 0, prev1, pltpu.roll(v, 1, axis=0))
    v2 = jnp.where(row == 0, prev2, jnp.where(row == 1, prev1, pltpu.roll(v, 2, axis=0)))
    cw = cw_ref[...]
    conv = cw[0:1, :] * v2 + cw[1:2, :] * v1 + cw[2:3, :] * v
    y = b_ref[...] * conv
    ms = jnp.mean(y * y, axis=-1, keepdims=True)
    o_ref[...] = (y * lax.rsqrt(ms + RMS_EPS) * g_ref[...]).astype(o_ref.dtype)


def _conv(hbc, conv_w, g_conv, layer):
    t = hbc.shape[0]
    halo_blocks = CONV_TM // SUBLANES

    def halo_map(col):
        return lambda i: (jnp.maximum(i * halo_blocks - 1, 0), col)

    return pl.pallas_call(
        _conv_kernel,
        out_shape=jax.ShapeDtypeStruct((t, D_CONV), BF16),
        grid=(t // CONV_TM,),
        in_specs=[
            pl.BlockSpec((CONV_TM, D_CONV), lambda i: (i, 0)),
            pl.BlockSpec((CONV_TM, D_CONV), lambda i: (i, 1)),
            pl.BlockSpec((CONV_TM, D_CONV), lambda i: (i, 2)),
            pl.BlockSpec((SUBLANES, D_CONV), halo_map(0)),
            pl.BlockSpec((SUBLANES, D_CONV), halo_map(2)),
            pl.BlockSpec((None, 3, D_CONV), lambda i: (layer, 0, 0)),
            pl.BlockSpec((None, 1, D_CONV), lambda i: (layer, 0, 0)),
        ],
        out_specs=pl.BlockSpec((CONV_TM, D_CONV), lambda i: (i, 0)),
        compiler_params=_params("arbitrary"),
        name="conv_rms",
    )(hbc, hbc, hbc, hbc, hbc, conv_w, g_conv.reshape(DEPTH, 1, D_CONV))


def _attn_kernel(q_ref, k_ref, v_ref, o_ref, *, bq, bk):
    i = pl.program_id(1)
    q = q_ref[...]
    row = lax.broadcasted_iota(I32, (bq, bk), 0)
    col = lax.broadcasted_iota(I32, (bq, bk), 1)
    causal = col < row
    r2 = lax.broadcasted_iota(I32, (bk, bk + LANES), 0)
    c2 = lax.broadcasted_iota(I32, (bk, bk + LANES), 1)
    tri = jnp.where((r2 >= c2) | (c2 >= bk), 1.0, 0.0).astype(BF16)

    def block(j, later, diagonal):
        start = pl.multiple_of(j * bk, bk)
        ks = k_ref[pl.ds(start, bk), :]
        vs = v_ref[pl.ds(start, bk), :]
        z = lax.dot_general(q, ks, (((1,), (1,)), ((), ())), preferred_element_type=F32)
        log_keep = -(jnp.maximum(z, 0.0) + jnp.log(1.0 + jnp.exp(-jnp.abs(z))))
        if diagonal:
            log_keep = jnp.where(causal, log_keep, 0.0)
        hi = log_keep.astype(BF16)
        lo = (log_keep - hi.astype(F32)).astype(BF16)
        sums = (jnp.dot(hi, tri, preferred_element_type=F32)
                + jnp.dot(lo, tri, preferred_element_type=F32))
        incl = sums[:, :bk] + jnp.concatenate([later] * (bk // LANES), axis=1)
        a = jnp.exp(z + incl)
        if diagonal:
            a = jnp.where(causal, a, 0.0)
        pv = jnp.dot(a.astype(BF16), vs, preferred_element_type=F32)
        return pv, later + sums[:, bk:]

    acc, later = block(i, jnp.zeros((bq, LANES), F32), True)

    def cond(c):
        jj, _, _, live = c
        return jnp.logical_and(jj <= i, live)

    def body(c):
        jj, acc, later, _ = c
        pv, later = block(i - jj, later, False)
        return jj + 1, acc + pv, later, jnp.max(later) > EXP_UNDERFLOW

    _, acc, _, _ = lax.while_loop(cond, body, (jnp.int32(1), acc, later, jnp.max(later) > EXP_UNDERFLOW))
    o_ref[...] = acc


def _attention(qkv):
    t = qkv.shape[0]
    bq, bk = ATT_BQ, ATT_BK
    assert bq == bk and HEAD_DIM == LANES
    return pl.pallas_call(
        functools.partial(_attn_kernel, bq=bq, bk=bk),
        out_shape=jax.ShapeDtypeStruct((t, D_ATT), F32),
        grid=(N_HEADS, t // bq),
        in_specs=[
            pl.BlockSpec((bq, HEAD_DIM), lambda h, i: (i, h)),
            pl.BlockSpec((t, HEAD_DIM), lambda h, i: (0, N_HEADS + h)),
            pl.BlockSpec((t, HEAD_DIM), lambda h, i: (0, 2 * N_HEADS + h)),
        ],
        out_specs=pl.BlockSpec((bq, HEAD_DIM), lambda h, i: (i, h)),
        compiler_params=_params("arbitrary", "arbitrary"),
        name="stickbreak_attn",
    )(qkv, qkv, qkv)


def _layer_norm(r, g, b):
    mu = jnp.mean(r, axis=-1, keepdims=True)
    rc = r - mu
    var = jnp.mean(rc * rc, axis=-1, keepdims=True)
    return rc * lax.rsqrt(var + LN_EPS) * g + b


def _outproj_kernel(yc_ref, att_ref, ga_ref, wc_ref, wa_ref, x_ref, g1_ref, lg_ref, lb_ref, o_ref):
    att = att_ref[...]
    ms = jnp.mean(att * att, axis=-1, keepdims=True)
    ya = (att * lax.rsqrt(ms + RMS_EPS) * ga_ref[...]).astype(BF16)
    mix = (jnp.dot(yc_ref[...], wc_ref[...], preferred_element_type=F32)
           + jnp.dot(ya, wa_ref[...], preferred_element_type=F32))
    r = DEEPNORM_ALPHA * x_ref[...] + (1.0 + g1_ref[...]) * mix
    o_ref[...] = _layer_norm(r, lg_ref[...], lb_ref[...])


def _outproj(yc, att, g_att, w_out_bf, x, g1, ln_g, ln_b, layer):
    t = x.shape[0]
    vec = lambda n: pl.BlockSpec((None, 1, n), lambda i: (layer, 0, 0))
    return pl.pallas_call(
        _outproj_kernel,
        out_shape=jax.ShapeDtypeStruct((t, D_MODEL), F32),
        grid=(t // OUT_TM,),
        in_specs=[
            pl.BlockSpec((OUT_TM, D_CONV), lambda i: (i, 0)),
            pl.BlockSpec((OUT_TM, D_ATT), lambda i: (i, 0)),
            vec(D_ATT),
            pl.BlockSpec((None, D_CONV, D_MODEL), lambda i: (layer, 0, 0)),
            pl.BlockSpec((None, D_ATT, D_MODEL), lambda i: (layer, 1, 0)),
            pl.BlockSpec((OUT_TM, D_MODEL), lambda i: (i, 0)),
            pl.BlockSpec((1, D_MODEL), lambda i: (0, 0)),
            vec(D_MODEL),
            vec(D_MODEL),
        ],
        out_specs=pl.BlockSpec((OUT_TM, D_MODEL), lambda i: (i, 0)),
        compiler_params=_params("arbitrary"),
        name="outproj_ln",
    )(yc, att, g_att.reshape(DEPTH, 1, D_ATT), w_out_bf, w_out_bf, x, g1,
      ln_g.reshape(DEPTH, 1, D_MODEL), ln_b.reshape(DEPTH, 1, D_MODEL))


def _split_bf16(a):
    hi = a.astype(BF16)
    lo = (a - hi.astype(F32)).astype(BF16)
    return hi, lo


def _pack_halves(a_bf):
    n = a_bf.shape[1] // 2
    bits = lax.bitcast_convert_type(a_bf.astype(F32), U32)
    return (bits[:, :n] >> 16) | (bits[:, n:] & jnp.uint32(0xFFFF0000))


def _unpack_halves(words):
    lo = lax.bitcast_convert_type(words << 16, F32).astype(BF16)
    hi = lax.bitcast_convert_type(words & jnp.uint32(0xFFFF0000), F32).astype(BF16)
    return lo, hi


def _router_kernel(x_ref, sc_ref, sh_ref, wr_ref, bias_ref,
                   u_ref, eidx_ref, wts_ref, rank_ref, cnt_ref):
    step = pl.program_id(0)
    tm = x_ref.shape[0]
    neg_inf = -jnp.inf

    @pl.when(step == 0)
    def _():
        cnt_ref[...] = jnp.zeros_like(cnt_ref)

    u = x_ref[...] * (1.0 + sc_ref[...]) + sh_ref[...]
    u_hi, u_lo = _split_bf16(u)
    u_ref[...] = _pack_halves(u_hi)
    w_hi, w_lo = _split_bf16(wr_ref[...])
    nt = (((1,), (1,)), ((), ()))
    logits = (lax.dot_general(w_hi, u_hi, nt, preferred_element_type=F32)
              + lax.dot_general(w_hi, u_lo, nt, preferred_element_type=F32)
              + lax.dot_general(w_lo, u_hi, nt, preferred_element_type=F32))
    scores = 1.0 / (1.0 + jnp.exp(-logits))
    sel = scores + bias_ref[...]

    gscore = []
    for g in range(N_GROUPS):
        grp = sel[g * GROUP_SIZE:(g + 1) * GROUP_SIZE, :]
        m1 = jnp.max(grp, axis=0, keepdims=True)
        is_max = grp == m1
        n_max = jnp.sum(is_max.astype(F32), axis=0, keepdims=True)
        rest = jnp.max(jnp.where(is_max, neg_inf, grp), axis=0, keepdims=True)
        gscore.append(m1 + jnp.where(n_max >= 2.0, m1, rest))
    masked = []
    for g in range(N_GROUPS):
        beaten = jnp.zeros((1, tm), F32)
        for o in range(N_GROUPS):
            if o == g:
                continue
            wins = (gscore[o] >= gscore[g]) if o < g else (gscore[o] > gscore[g])
            beaten = beaten + wins.astype(F32)
        keep = beaten < float(TOPK_GROUPS)
        grp = sel[g * GROUP_SIZE:(g + 1) * GROUP_SIZE, :]
        masked.append(jnp.where(keep, grp, neg_inf))
    cand = jnp.concatenate(masked, axis=0)

    eiota = lax.broadcasted_iota(I32, (N_EXPERTS, tm), 0)
    picked = jnp.zeros((N_EXPERTS, tm), F32)
    idxs, ws = [], []
    for _ in range(TOP_K):
        m = jnp.max(cand, axis=0, keepdims=True)
        idx = jnp.min(jnp.where(cand == m, eiota, N_EXPERTS), axis=0, keepdims=True)
        hit = eiota == idx
        ws.append(jnp.sum(jnp.where(hit, scores, 0.0), axis=0, keepdims=True))
        idxs.append(idx)
        picked = jnp.where(hit, 1.0, picked)
        cand = jnp.where(hit, neg_inf, cand)
    wsum = ws[0]
    for k in range(1, TOP_K):
        wsum = wsum + ws[k]
    norm = ROUTED_SCALE / wsum

    r = lax.broadcasted_iota(I32, (tm, tm), 0)
    c = lax.broadcasted_iota(I32, (tm, tm), 1)
    before = jnp.where(r < c, 1.0, 0.0).astype(BF16)
    ones = jnp.ones((tm, LANES), BF16)
    picked_bf = picked.astype(BF16)
    prior = jnp.dot(picked_bf, before, preferred_element_type=F32)
    total = jnp.dot(picked_bf, ones, preferred_element_type=F32)
    carried = cnt_ref[...]
    rank_full = prior + jnp.concatenate([carried] * (tm // LANES), axis=1)
    cnt_ref[...] = carried + total

    for k in range(TOP_K):
        hit = eiota == idxs[k]
        eidx_ref[k:k + 1, :] = idxs[k]
        wts_ref[k:k + 1, :] = ws[k] * norm
        rank_ref[k:k + 1, :] = jnp.sum(jnp.where(hit, rank_full, 0.0), axis=0,
                                       keepdims=True).astype(I32)


def _router(x, sc, sh, w_router_t, router_bias, layer):
    t = x.shape[0]
    tm = ROUTER_TM
    return pl.pallas_call(
        _router_kernel,
        out_shape=(
            jax.ShapeDtypeStruct((t, D_MODEL // 2), U32),
            jax.ShapeDtypeStruct((TOP_K, t), I32),
            jax.ShapeDtypeStruct((TOP_K, t), F32),
            jax.ShapeDtypeStruct((TOP_K, t), I32),
            jax.ShapeDtypeStruct((N_EXPERTS, LANES), F32),
        ),
        grid=(t // tm,),
        in_specs=[
            pl.BlockSpec((tm, D_MODEL), lambda i: (i, 0)),
            pl.BlockSpec((1, D_MODEL), lambda i: (0, 0)),
            pl.BlockSpec((1, D_MODEL), lambda i: (0, 0)),
            pl.BlockSpec((None, N_EXPERTS, D_MODEL), lambda i: (layer, 0, 0)),
            pl.BlockSpec((None, N_EXPERTS, 1), lambda i: (layer, 0, 0)),
        ],
        out_specs=(
            pl.BlockSpec((tm, D_MODEL // 2), lambda i: (i, 0)),
            pl.BlockSpec((TOP_K, tm), lambda i: (0, i)),
            pl.BlockSpec((TOP_K, tm), lambda i: (0, i)),
            pl.BlockSpec((TOP_K, tm), lambda i: (0, i)),
            pl.BlockSpec((N_EXPERTS, LANES), lambda i: (0, 0)),
        ),
        compiler_params=_params("arbitrary"),
        name="router",
    )(x, sc, sh, w_router_t, router_bias.reshape(DEPTH, N_EXPERTS, 1))


def _dest_kernel(pad_start_ref, eidx_ref, rank_ref, dest_ref):
    eidx = eidx_ref[...]
    dest = rank_ref[...]
    for e in range(N_EXPERTS):
        dest = dest + jnp.where(eidx == e, pad_start_ref[e], 0)
    dest_ref[...] = dest


def _dest_rows(pad_start, eidx, rank):
    k, t = eidx.shape
    grid_spec = pltpu.PrefetchScalarGridSpec(
        num_scalar_prefetch=1,
        grid=(1,),
        in_specs=[pl.BlockSpec((k, t), lambda i, p: (0, 0)), pl.BlockSpec((k, t), lambda i, p: (0, 0))],
        out_specs=pl.BlockSpec((k, t), lambda i, p: (0, 0)),
    )
    return pl.pallas_call(
        _dest_kernel,
        out_shape=jax.ShapeDtypeStruct((k, t), I32),
        grid_spec=grid_spec,
        compiler_params=_params("arbitrary"),
        name="dest_rows",
    )(pad_start, eidx, rank)


def _row_copy(src, src_row, dst, dst_row, sem):
    return pltpu.make_async_copy(src.at[pl.ds(src_row, 1), :], dst.at[pl.ds(dst_row, 1), :], sem)


def _block_copy(src, dst_hbm, blk, sem):
    rows = src.shape[0]
    return pltpu.make_async_copy(src, dst_hbm.at[pl.ds(pl.multiple_of(blk * rows, rows), rows), :], sem)


def _dispatch_kernel(dest_ref, zflag_ref, u_ref, xg_hbm, zbuf, sem, zsem, *, n_tok, n_blocks):
    i = pl.program_id(0)
    tm = u_ref.shape[0]

    @pl.when(i == 0)
    def _():
        zbuf[...] = jnp.zeros_like(zbuf)

        def zero_start(b, _):
            @pl.when(zflag_ref[b] != 0)
            def _():
                _block_copy(zbuf, xg_hbm, b, zsem).start()
            return 0

        def zero_wait(b, _):
            @pl.when(zflag_ref[b] != 0)
            def _():
                _block_copy(zbuf, xg_hbm, b, zsem).wait()
            return 0

        lax.fori_loop(0, n_blocks, zero_start, 0)
        lax.fori_loop(0, n_blocks, zero_wait, 0)

    for k in range(TOP_K):
        base = k * n_tok + i * tm

        def issue(r, _, base=base):
            _row_copy(u_ref, r, xg_hbm, dest_ref[base + r], sem).start()
            return 0

        lax.fori_loop(0, tm, issue, 0, unroll=DMA_UNROLL)

    def drain(r, _):
        _row_copy(u_ref, 0, xg_hbm, 0, sem).wait()
        return 0

    lax.fori_loop(0, TOP_K * tm, drain, 0, unroll=DMA_UNROLL)


def _dispatch(dest_flat, zflag, u_packed, n_blocks):
    t, w = u_packed.shape
    tm = DISPATCH_TM
    grid_spec = pltpu.PrefetchScalarGridSpec(
        num_scalar_prefetch=2,
        grid=(t // tm,),
        in_specs=[pl.BlockSpec((tm, w), lambda i, d, z: (i, 0))],
        out_specs=pl.BlockSpec(memory_space=pl.ANY),
        scratch_shapes=[
            pltpu.VMEM((EXPERT_BLOCK, w), U32),
            pltpu.SemaphoreType.DMA(()),
            pltpu.SemaphoreType.DMA(()),
        ],
    )
    return pl.pallas_call(
        functools.partial(_dispatch_kernel, n_tok=t, n_blocks=n_blocks),
        out_shape=jax.ShapeDtypeStruct((n_blocks * EXPERT_BLOCK, w), U32),
        grid_spec=grid_spec,
        compiler_params=_params("arbitrary"),
        name="dispatch",
    )(dest_flat, zflag, u_packed)


def _swiglu(lo, hi, wg_ref, wu_ref, wd_ref):
    half = D_MODEL // 2
    g = (jnp.dot(lo, wg_ref[:half, :], preferred_element_type=F32)
         + jnp.dot(hi, wg_ref[half:, :], preferred_element_type=F32))
    up = (jnp.dot(lo, wu_ref[:half, :], preferred_element_type=F32)
          + jnp.dot(hi, wu_ref[half:, :], preferred_element_type=F32))
    h = (g * (1.0 / (1.0 + jnp.exp(-g))) * up).astype(BF16)
    return jnp.dot(h, wd_ref[...], preferred_element_type=F32)


def _expert_kernel(be_ref, nused_ref, x_ref, wg_ref, wu_ref, wd_ref, o_ref):
    b = pl.program_id(0)

    @pl.when(b < nused_ref[0])
    def _():
        lo, hi = _unpack_halves(x_ref[...])
        o_ref[...] = _swiglu(lo, hi, wg_ref, wu_ref, wd_ref)

    @pl.when(b >= nused_ref[0])
    def _():
        o_ref[...] = jnp.zeros_like(o_ref)


def _experts(blk_expert, n_used, xg, wg_bf, wu_bf, wd_bf, layer, n_blocks):
    n_rows, w = xg.shape
    wspec = lambda r, c: pl.BlockSpec((None, None, r, c), lambda b, be, nu: (layer, be[b], 0, 0))
    grid_spec = pltpu.PrefetchScalarGridSpec(
        num_scalar_prefetch=2,
        grid=(n_blocks,),
        in_specs=[
            pl.BlockSpec((EXPERT_BLOCK, w), lambda b, be, nu: (jnp.minimum(b, nu[0] - 1), 0)),
            wspec(D_MODEL, D_EXPERT), wspec(D_MODEL, D_EXPERT), wspec(D_EXPERT, D_MODEL),
        ],
        out_specs=pl.BlockSpec((EXPERT_BLOCK, D_MODEL), lambda b, be, nu: (b, 0)),
    )
    return pl.pallas_call(
        _expert_kernel,
        out_shape=jax.ShapeDtypeStruct((n_rows, D_MODEL), F32),
        grid_spec=grid_spec,
        compiler_params=_params("arbitrary"),
        name="experts",
    )(blk_expert, n_used, xg, wg_bf, wu_bf, wd_bf)


def _combine_kernel(dest_ref, eo_hbm, wts_ref, x_ref, sc_ref, sh_ref, g2_ref,
                    sg_ref, su_ref, sd_ref, lg_ref, lb_ref, o_ref, gbuf, sem, *, n_tok):
    i = pl.program_id(0)
    tm = x_ref.shape[0]

    for k in range(TOP_K):
        base = k * n_tok + i * tm

        def issue(r, _, k=k, base=base):
            _row_copy(eo_hbm, dest_ref[base + r], gbuf.at[k], r, sem).start()
            return 0

        lax.fori_loop(0, tm, issue, 0, unroll=DMA_UNROLL)

    x = x_ref[...]
    u = (x * (1.0 + sc_ref[...]) + sh_ref[...]).astype(BF16)
    half = D_MODEL // 2
    ffn = _swiglu(u[:, :half], u[:, half:], sg_ref, su_ref, sd_ref)

    def drain(r, _):
        _row_copy(eo_hbm, 0, gbuf.at[0], 0, sem).wait()
        return 0

    lax.fori_loop(0, TOP_K * tm, drain, 0, unroll=DMA_UNROLL)

    wts = wts_ref[...]
    for k in range(TOP_K):
        ffn = ffn + wts[:, k:k + 1] * gbuf[k]
    r = DEEPNORM_ALPHA * x + (1.0 + g2_ref[...]) * ffn
    o_ref[...] = _layer_norm(r, lg_ref[...], lb_ref[...])


def _combine(dest_flat, eo, wts_tk, x, sc, sh, g2, sg_bf, su_bf, sd_bf, ln_g, ln_b, layer):
    t = x.shape[0]
    tm = COMBINE_TM
    vec = lambda n: pl.BlockSpec((None, 1, n), lambda i, d: (layer, 0, 0))
    row = lambda: pl.BlockSpec((1, D_MODEL), lambda i, d: (0, 0))
    grid_spec = pltpu.PrefetchScalarGridSpec(
        num_scalar_prefetch=1,
        grid=(t // tm,),
        in_specs=[
            pl.BlockSpec(memory_space=pl.ANY),
            pl.BlockSpec((tm, TOP_K), lambda i, d: (i, 0)),
            pl.BlockSpec((tm, D_MODEL), lambda i, d: (i, 0)),
            row(), row(), row(),
            pl.BlockSpec((None, D_MODEL, D_EXPERT), lambda i, d: (layer, 0, 0)),
            pl.BlockSpec((None, D_MODEL, D_EXPERT), lambda i, d: (layer, 0, 0)),
            pl.BlockSpec((None, D_EXPERT, D_MODEL), lambda i, d: (layer, 0, 0)),
            vec(D_MODEL), vec(D_MODEL),
        ],
        out_specs=pl.BlockSpec((tm, D_MODEL), lambda i, d: (i, 0)),
        scratch_shapes=[
            pltpu.VMEM((TOP_K, tm, D_MODEL), F32),
            pltpu.SemaphoreType.DMA(()),
        ],
    )
    return pl.pallas_call(
        functools.partial(_combine_kernel, n_tok=t),
        out_shape=jax.ShapeDtypeStruct((t, D_MODEL), F32),
        grid_spec=grid_spec,
        compiler_params=_params("arbitrary"),
        name="combine_ln",
    )(dest_flat, eo, wts_tk, x, sc, sh, g2, sg_bf, su_bf, sd_bf,
      ln_g.reshape(DEPTH, 1, D_MODEL), ln_b.reshape(DEPTH, 1, D_MODEL))


def _block_plan(counts, n_blocks):
    blocks_per_e = (counts + EXPERT_BLOCK - 1) // EXPERT_BLOCK
    blk_end = jnp.cumsum(blocks_per_e)
    pad_start = ((blk_end - blocks_per_e) * EXPERT_BLOCK).astype(I32)
    n_used = blk_end[-1:].astype(I32)
    blk = jnp.arange(n_blocks, dtype=I32)[:, None]
    blk_expert = jnp.minimum(jnp.sum((blk_end[None, :] <= blk).astype(I32), axis=1), N_EXPERTS - 1)
    partial_blk = jnp.any((blk == blk_end[None, :] - 1) & (blocks_per_e[None, :] > 0), axis=1)
    zflag = (partial_blk | (blk[:, 0] >= n_used[0])).astype(I32)
    return pad_start, blk_expert.astype(I32), n_used, zflag


def kernel(x, c, w_mod, b_mod, w_in, conv_w, g_conv, g_att, w_out, ln1_g, ln1_b, w_router,
           router_bias, w_gate, w_up, w_down, ws_gate, ws_up, ws_down, ln2_g, ln2_b):
    bsz, seq, d = x.shape
    assert bsz == 1 and d == D_MODEL
    t = bsz * seq
    n_blocks = -(-(t * TOP_K) // EXPERT_BLOCK) + N_EXPERTS
    xt = x.reshape(t, d)

    mod = _modulation(c, w_mod, b_mod)
    w_in_bf = w_in.astype(BF16)
    w_out_bf = w_out.astype(BF16)
    wg_bf, wu_bf, wd_bf = w_gate.astype(BF16), w_up.astype(BF16), w_down.astype(BF16)
    sg_bf, su_bf, sd_bf = ws_gate.astype(BF16), ws_up.astype(BF16), ws_down.astype(BF16)
    w_router_t = jnp.swapaxes(w_router, 1, 2)

    for l in range(DEPTH):
        sh1, sc1, g1, sh2, sc2, g2 = [mod[l, :, n * d:(n + 1) * d] for n in range(N_MOD)]
        hbc = _inproj(xt, sc1, sh1, w_in_bf, l, 0, F32, 0)
        qkv = _inproj(xt, sc1, sh1, w_in_bf, l, 3 * D_CONV, BF16, D_ATT // INPROJ_TN)
        yc = _conv(hbc, conv_w, g_conv, l)
        att = _attention(qkv)
        xt = _outproj(yc, att, g_att, w_out_bf, xt, g1, ln1_g, ln1_b, l)
        u_packed, eidx, wts, rank, cnt = _router(xt, sc2, sh2, w_router_t, router_bias, l)
        pad_start, blk_expert, n_used, zflag = _block_plan(cnt[:, 0].astype(I32), n_blocks)
        dest = _dest_rows(pad_start, eidx, rank).reshape(-1)
        xg = _dispatch(dest, zflag, u_packed, n_blocks)
        eo = _experts(blk_expert, n_used, xg, wg_bf, wu_bf, wd_bf, l, n_blocks)
        xt = _combine(dest, eo, wts.T, xt, sc2, sh2, g2, sg_bf, su_bf, sd_bf, ln2_g, ln2_b, l)
    return xt.reshape(bsz, seq, d)
```

```python
import functools

import jax
import jax.numpy as jnp
from jax import lax
from jax.experimental import pallas as pl
from jax.experimental.pallas import tpu as pltpu

F32 = jnp.float32
BF16 = jnp.bfloat16
I32 = jnp.int32
U32 = jnp.uint32

D_MODEL = 2048
DEPTH = 4
D_CONV = D_MODEL // 2
D_ATT = D_MODEL // 2
HEAD_DIM = 128
N_HEADS = D_ATT // HEAD_DIM
N_EXPERTS = 64
TOP_K = 8
N_GROUPS = 8
GROUP_SIZE = N_EXPERTS // N_GROUPS
TOPK_GROUPS = 4
D_EXPERT = 384
ROUTED_SCALE = 2.5
EXPERT_BLOCK = 256
DEEPNORM_ALPHA = (2 * DEPTH) ** 0.25
LN_EPS = 1e-5
RMS_EPS = 1e-6
N_MOD = 6
ATT_SCALE = HEAD_DIM ** -0.5
EXP_UNDERFLOW = -110.0

LANES = 128
SUBLANES = 8
VMEM_LIMIT = 48 * 1024 * 1024

MOD_TN = 1024
INPROJ_TM = 512
INPROJ_TN = 1024
CONV_TM = 512
ATT_BQ = 256
ATT_BK = 256
OUT_TM = 256
ROUTER_TM = 512
COMBINE_TM = 256
DISPATCH_TM = 256
DMA_UNROLL = 8
TOKEN_ROWS = (D_MODEL // 2) // LANES


def _params(*sem):
    return pltpu.CompilerParams(dimension_semantics=sem, vmem_limit_bytes=VMEM_LIMIT)


def _mod_kernel(c_ref, w_ref, b_ref, o_ref):
    o_ref[...] = jnp.dot(c_ref[...], w_ref[...], preferred_element_type=F32) + b_ref[...]


def _modulation(c, w_mod, b_mod):
    n = N_MOD * D_MODEL
    c8 = jnp.broadcast_to(c, (SUBLANES, D_MODEL))
    out = pl.pallas_call(
        _mod_kernel,
        out_shape=jax.ShapeDtypeStruct((DEPTH, SUBLANES, n), F32),
        grid=(DEPTH, n // MOD_TN),
        in_specs=[
            pl.BlockSpec((SUBLANES, D_MODEL), lambda l, j: (0, 0)),
            pl.BlockSpec((None, D_MODEL, MOD_TN), lambda l, j: (l, 0, j)),
            pl.BlockSpec((None, 1, MOD_TN), lambda l, j: (l, 0, j)),
        ],
        out_specs=pl.BlockSpec((None, SUBLANES, MOD_TN), lambda l, j: (l, 0, j)),
        compiler_params=_params("arbitrary", "arbitrary"),
        name="modulation",
    )(c8, w_mod, b_mod.reshape(DEPTH, 1, n))
    return out[:, 0:1, :]


def _inproj_kernel(x_ref, sc_ref, sh_ref, w_ref, o_ref, *, q_tiles):
    u = (x_ref[...] * (1.0 + sc_ref[...]) + sh_ref[...]).astype(BF16)
    acc = jnp.dot(u, w_ref[...], preferred_element_type=F32)
    if q_tiles:
        acc = acc * jnp.where(pl.program_id(0) < q_tiles, ATT_SCALE, 1.0)
    o_ref[...] = acc.astype(o_ref.dtype)


def _inproj(x, sc, sh, w_bf, layer, col0, out_dtype, q_tiles):
    t = x.shape[0]
    n = 3 * D_CONV
    tile0 = col0 // INPROJ_TN
    return pl.pallas_call(
        functools.partial(_inproj_kernel, q_tiles=q_tiles),
        out_shape=jax.ShapeDtypeStruct((t, n), out_dtype),
        grid=(n // INPROJ_TN, t // INPROJ_TM),
        in_specs=[
            pl.BlockSpec((INPROJ_TM, D_MODEL), lambda j, i: (i, 0)),
            pl.BlockSpec((1, D_MODEL), lambda j, i: (0, 0)),
            pl.BlockSpec((1, D_MODEL), lambda j, i: (0, 0)),
            pl.BlockSpec((None, D_MODEL, INPROJ_TN), lambda j, i: (layer, 0, tile0 + j)),
        ],
        out_specs=pl.BlockSpec((INPROJ_TM, INPROJ_TN), lambda j, i: (i, j)),
        compiler_params=_params("arbitrary", "arbitrary"),
        name="inproj",
    )(x, sc, sh, w_bf)


def _conv_kernel(h_ref, b_ref, c_ref, hh_ref, ch_ref, cw_ref, g_ref, o_ref):
    i = pl.program_id(0)
    v = c_ref[...] * h_ref[...]
    halo = jnp.where(i > 0, ch_ref[...] * hh_ref[...], 0.0)
    row = lax.broadcasted_iota(I32, v.shape, 0)
    prev1 = halo[SUBLANES - 1:SUBLANES, :]
    prev2 = halo[SUBLANES - 2:SUBLANES - 1, :]
    v1 = jnp.where(row == 0, prev1, pltpu.roll(v, 1, axis=0))
    v2 = jnp.where(row == 0, prev2, jnp.where(row == 1, prev1, pltpu.roll(v, 2, axis=0)))
    cw = cw_ref[...]
    conv = cw[0:1, :] * v2 + cw[1:2, :] * v1 + cw[2:3, :] * v
    y = b_ref[...] * conv
    ms = jnp.mean(y * y, axis=-1, keepdims=True)
    o_ref[...] = (y * lax.rsqrt(ms + RMS_EPS) * g_ref[...]).astype(o_ref.dtype)


def _conv(hbc, conv_w, g_conv, layer):
    t = hbc.shape[0]
    halo_blocks = CONV_TM // SUBLANES

    def halo_map(col):
        return lambda i: (jnp.maximum(i * halo_blocks - 1, 0), col)

    return pl.pallas_call(
        _conv_kernel,
        out_shape=jax.ShapeDtypeStruct((t, D_CONV), BF16),
        grid=(t // CONV_TM,),
        in_specs=[
            pl.BlockSpec((CONV_TM, D_CONV), lambda i: (i, 0)),
            pl.BlockSpec((CONV_TM, D_CONV), lambda i: (i, 1)),
            pl.BlockSpec((CONV_TM, D_CONV), lambda i: (i, 2)),
            pl.BlockSpec((SUBLANES, D_CONV), halo_map(0)),
            pl.BlockSpec((SUBLANES, D_CONV), halo_map(2)),
            pl.BlockSpec((None, 3, D_CONV), lambda i: (layer, 0, 0)),
            pl.BlockSpec((None, 1, D_CONV), lambda i: (layer, 0, 0)),
        ],
        out_specs=pl.BlockSpec((CONV_TM, D_CONV), lambda i: (i, 0)),
        compiler_params=_params("arbitrary"),
        name="conv_rms",
    )(hbc, hbc, hbc, hbc, hbc, conv_w, g_conv.reshape(DEPTH, 1, D_CONV))


def _attn_kernel(q_ref, k_ref, v_ref, o_ref, *, bq, bk):
    i = pl.program_id(1)
    q = q_ref[...]
    row = lax.broadcasted_iota(I32, (bq, bk), 0)
    col = lax.broadcasted_iota(I32, (bq, bk), 1)
    causal = col < row
    r2 = lax.broadcasted_iota(I32, (bk, bk + LANES), 0)
    c2 = lax.broadcasted_iota(I32, (bk, bk + LANES), 1)
    tri = jnp.where((r2 >= c2) | (c2 >= bk), 1.0, 0.0).astype(BF16)

    def block(j, later, diagonal):
        start = pl.multiple_of(j * bk, bk)
        ks = k_ref[pl.ds(start, bk), :]
        vs = v_ref[pl.ds(start, bk), :]
        z = lax.dot_general(q, ks, (((1,), (1,)), ((), ())), preferred_element_type=F32)
        log_keep = -(jnp.maximum(z, 0.0) + jnp.log(1.0 + jnp.exp(-jnp.abs(z))))
        if diagonal:
            log_keep = jnp.where(causal, log_keep, 0.0)
        hi = log_keep.astype(BF16)
        lo = (log_keep - hi.astype(F32)).astype(BF16)
        sums = (jnp.dot(hi, tri, preferred_element_type=F32)
                + jnp.dot(lo, tri, preferred_element_type=F32))
        incl = sums[:, :bk] + jnp.concatenate([later] * (bk // LANES), axis=1)
        a = jnp.exp(z + incl)
        if diagonal:
            a = jnp.where(causal, a, 0.0)
        pv = jnp.dot(a.astype(BF16), vs, preferred_element_type=F32)
        return pv, later + sums[:, bk:]

    acc, later = block(i, jnp.zeros((bq, LANES), F32), True)

    def cond(c):
        jj, _, _, live = c
        return jnp.logical_and(jj <= i, live)

    def body(c):
        jj, acc, later, _ = c
        pv, later = block(i - jj, later, False)
        return jj + 1, acc + pv, later, jnp.max(later) > EXP_UNDERFLOW

    _, acc, _, _ = lax.while_loop(cond, body, (jnp.int32(1), acc, later, jnp.max(later) > EXP_UNDERFLOW))
    o_ref[...] = acc


def _attention(qkv):
    t = qkv.shape[0]
    bq, bk = ATT_BQ, ATT_BK
    assert bq == bk and HEAD_DIM == LANES
    return pl.pallas_call(
        functools.partial(_attn_kernel, bq=bq, bk=bk),
        out_shape=jax.ShapeDtypeStruct((t, D_ATT), F32),
        grid=(N_HEADS, t // bq),
        in_specs=[
            pl.BlockSpec((bq, HEAD_DIM), lambda h, i: (i, h)),
            pl.BlockSpec((t, HEAD_DIM), lambda h, i: (0, N_HEADS + h)),
            pl.BlockSpec((t, HEAD_DIM), lambda h, i: (0, 2 * N_HEADS + h)),
        ],
        out_specs=pl.BlockSpec((bq, HEAD_DIM), lambda h, i: (i, h)),
        compiler_params=_params("arbitrary", "arbitrary"),
        name="stickbreak_attn",
    )(qkv, qkv, qkv)


def _layer_norm(r, g, b):
    mu = jnp.mean(r, axis=-1, keepdims=True)
    rc = r - mu
    var = jnp.mean(rc * rc, axis=-1, keepdims=True)
    return rc * lax.rsqrt(var + LN_EPS) * g + b


def _outproj_kernel(yc_ref, att_ref, ga_ref, wc_ref, wa_ref, x_ref, g1_ref, lg_ref, lb_ref, o_ref):
    att = att_ref[...]
    ms = jnp.mean(att * att, axis=-1, keepdims=True)
    ya = (att * lax.rsqrt(ms + RMS_EPS) * ga_ref[...]).astype(BF16)
    mix = (jnp.dot(yc_ref[...], wc_ref[...], preferred_element_type=F32)
           + jnp.dot(ya, wa_ref[...], preferred_element_type=F32))
    r = DEEPNORM_ALPHA * x_ref[...] + (1.0 + g1_ref[...]) * mix
    o_ref[...] = _layer_norm(r, lg_ref[...], lb_ref[...])


def _outproj(yc, att, g_att, w_out_bf, x, g1, ln_g, ln_b, layer):
    t = x.shape[0]
    vec = lambda n: pl.BlockSpec((None, 1, n), lambda i: (layer, 0, 0))
    return pl.pallas_call(
        _outproj_kernel,
        out_shape=jax.ShapeDtypeStruct((t, D_MODEL), F32),
        grid=(t // OUT_TM,),
        in_specs=[
            pl.BlockSpec((OUT_TM, D_CONV), lambda i: (i, 0)),
            pl.BlockSpec((OUT_TM, D_ATT), lambda i: (i, 0)),
            vec(D_ATT),
            pl.BlockSpec((None, D_CONV, D_MODEL), lambda i: (layer, 0, 0)),
            pl.BlockSpec((None, D_ATT, D_MODEL), lambda i: (layer, 1, 0)),
            pl.BlockSpec((OUT_TM, D_MODEL), lambda i: (i, 0)),
            pl.BlockSpec((1, D_MODEL), lambda i: (0, 0)),
            vec(D_MODEL),
            vec(D_MODEL),
        ],
        out_specs=pl.BlockSpec((OUT_TM, D_MODEL), lambda i: (i, 0)),
        compiler_params=_params("arbitrary"),
        name="outproj_ln",
    )(yc, att, g_att.reshape(DEPTH, 1, D_ATT), w_out_bf, w_out_bf, x, g1,
      ln_g.reshape(DEPTH, 1, D_MODEL), ln_b.reshape(DEPTH, 1, D_MODEL))


def _split_bf16(a):
    hi = a.astype(BF16)
    lo = (a - hi.astype(F32)).astype(BF16)
    return hi, lo


def _pack_halves(a_bf):
    n = a_bf.shape[1] // 2
    bits = lax.bitcast_convert_type(a_bf.astype(F32), U32)
    return (bits[:, :n] >> 16) | (bits[:, n:] & jnp.uint32(0xFFFF0000))


def _unpack_halves(words):
    lo = lax.bitcast_convert_type(words << 16, F32)
    hi = lax.bitcast_convert_type(words & jnp.uint32(0xFFFF0000), F32)
    return lo, hi


def _store_token_tiles(ref, words):
    m = words.shape[0]
    for g in range(TOKEN_ROWS):
        ref[pl.ds(g, m, stride=TOKEN_ROWS), :] = words[:, g * LANES:(g + 1) * LANES]


def _load_token_tiles(ref, m):
    return jnp.concatenate(
        [ref[pl.ds(g, m, stride=TOKEN_ROWS), :] for g in range(TOKEN_ROWS)], axis=1)


def _router_kernel(x_ref, sc_ref, sh_ref, wr_ref, bias_ref,
                   u_ref, eidx_ref, wts_ref, rank_ref, cnt_ref):
    step = pl.program_id(0)
    tm = x_ref.shape[0]
    neg_inf = -jnp.inf

    @pl.when(step == 0)
    def _():
        cnt_ref[...] = jnp.zeros_like(cnt_ref)

    u = x_ref[...] * (1.0 + sc_ref[...]) + sh_ref[...]
    u_hi, u_lo = _split_bf16(u)
    _store_token_tiles(u_ref, _pack_halves(u_hi))
    w_hi, w_lo = _split_bf16(wr_ref[...])
    nt = (((1,), (1,)), ((), ()))
    logits = (lax.dot_general(w_hi, u_hi, nt, preferred_element_type=F32)
              + lax.dot_general(w_hi, u_lo, nt, preferred_element_type=F32)
              + lax.dot_general(w_lo, u_hi, nt, preferred_element_type=F32))
    scores = 1.0 / (1.0 + jnp.exp(-logits))
    sel = scores + bias_ref[...]

    gscore = []
    for g in range(N_GROUPS):
        grp = sel[g * GROUP_SIZE:(g + 1) * GROUP_SIZE, :]
        m1 = jnp.max(grp, axis=0, keepdims=True)
        is_max = grp == m1
        n_max = jnp.sum(is_max.astype(F32), axis=0, keepdims=True)
        rest = jnp.max(jnp.where(is_max, neg_inf, grp), axis=0, keepdims=True)
        gscore.append(m1 + jnp.where(n_max >= 2.0, m1, rest))
    masked = []
    for g in range(N_GROUPS):
        beaten = jnp.zeros((1, tm), F32)
        for o in range(N_GROUPS):
            if o == g:
                continue
            wins = (gscore[o] >= gscore[g]) if o < g else (gscore[o] > gscore[g])
            beaten = beaten + wins.astype(F32)
        keep = beaten < float(TOPK_GROUPS)
        grp = sel[g * GROUP_SIZE:(g + 1) * GROUP_SIZE, :]
        masked.append(jnp.where(keep, grp, neg_inf))
    cand = jnp.concatenate(masked, axis=0)

    eiota = lax.broadcasted_iota(I32, (N_EXPERTS, tm), 0)
    picked = jnp.zeros((N_EXPERTS, tm), F32)
    idxs, ws = [], []
    for _ in range(TOP_K):
        m = jnp.max(cand, axis=0, keepdims=True)
        idx = jnp.min(jnp.where(cand == m, eiota, N_EXPERTS), axis=0, keepdims=True)
        hit = eiota == idx
        ws.append(jnp.sum(jnp.where(hit, scores, 0.0), axis=0, keepdims=True))
        idxs.append(idx)
        picked = jnp.where(hit, 1.0, picked)
        cand = jnp.where(hit, neg_inf, cand)
    wsum = ws[0]
    for k in range(1, TOP_K):
        wsum = wsum + ws[k]
    norm = ROUTED_SCALE / wsum

    r = lax.broadcasted_iota(I32, (tm, tm), 0)
    c = lax.broadcasted_iota(I32, (tm, tm), 1)
    before = jnp.where(r < c, 1.0, 0.0).astype(BF16)
    ones = jnp.ones((tm, LANES), BF16)
    picked_bf = picked.astype(BF16)
    prior = jnp.dot(picked_bf, before, preferred_element_type=F32)
    total = jnp.dot(picked_bf, ones, preferred_element_type=F32)
    carried = cnt_ref[...]
    rank_full = prior + jnp.concatenate([carried] * (tm // LANES), axis=1)
    cnt_ref[...] = carried + total

    for k in range(TOP_K):
        hit = eiota == idxs[k]
        eidx_ref[k:k + 1, :] = idxs[k]
        wts_ref[k:k + 1, :] = ws[k] * norm
        rank_ref[k:k + 1, :] = jnp.sum(jnp.where(hit, rank_full, 0.0), axis=0,
                                       keepdims=True).astype(I32)


def _router(x, sc, sh, w_router_t, router_bias, layer):
    t = x.shape[0]
    tm = ROUTER_TM
    return pl.pallas_call(
        _router_kernel,
        out_shape=(
            jax.ShapeDtypeStruct((t * TOKEN_ROWS, LANES), U32),
            jax.ShapeDtypeStruct((TOP_K, t), I32),
            jax.ShapeDtypeStruct((TOP_K, t), F32),
            jax.ShapeDtypeStruct((TOP_K, t), I32),
            jax.ShapeDtypeStruct((N_EXPERTS, LANES), F32),
        ),
        grid=(t // tm,),
        in_specs=[
            pl.BlockSpec((tm, D_MODEL), lambda i: (i, 0)),
            pl.BlockSpec((1, D_MODEL), lambda i: (0, 0)),
            pl.BlockSpec((1, D_MODEL), lambda i: (0, 0)),
            pl.BlockSpec((None, N_EXPERTS, D_MODEL), lambda i: (layer, 0, 0)),
            pl.BlockSpec((None, N_EXPERTS, 1), lambda i: (layer, 0, 0)),
        ],
        out_specs=(
            pl.BlockSpec((tm * TOKEN_ROWS, LANES), lambda i: (i, 0)),
            pl.BlockSpec((TOP_K, tm), lambda i: (0, i)),
            pl.BlockSpec((TOP_K, tm), lambda i: (0, i)),
            pl.BlockSpec((TOP_K, tm), lambda i: (0, i)),
            pl.BlockSpec((N_EXPERTS, LANES), lambda i: (0, 0)),
        ),
        compiler_params=_params("arbitrary"),
        name="router",
    )(x, sc, sh, w_router_t, router_bias.reshape(DEPTH, N_EXPERTS, 1))


def _dest_kernel(pad_start_ref, eidx_ref, rank_ref, dest_ref):
    eidx = eidx_ref[...]
    dest = rank_ref[...]
    for e in range(N_EXPERTS):
        dest = dest + jnp.where(eidx == e, pad_start_ref[e], 0)
    dest_ref[...] = dest


def _dest_rows(pad_start, eidx, rank):
    k, t = eidx.shape
    grid_spec = pltpu.PrefetchScalarGridSpec(
        num_scalar_prefetch=1,
        grid=(1,),
        in_specs=[pl.BlockSpec((k, t), lambda i, p: (0, 0)), pl.BlockSpec((k, t), lambda i, p: (0, 0))],
        out_specs=pl.BlockSpec((k, t), lambda i, p: (0, 0)),
    )
    return pl.pallas_call(
        _dest_kernel,
        out_shape=jax.ShapeDtypeStruct((k, t), I32),
        grid_spec=grid_spec,
        compiler_params=_params("arbitrary"),
        name="dest_rows",
    )(pad_start, eidx, rank)


def _token_copy(src, src_tok, dst, dst_tok, sem):
    s = pl.ds(pl.multiple_of(src_tok * TOKEN_ROWS, TOKEN_ROWS), TOKEN_ROWS)
    d = pl.ds(pl.multiple_of(dst_tok * TOKEN_ROWS, TOKEN_ROWS), TOKEN_ROWS)
    return pltpu.make_async_copy(src.at[s, :], dst.at[d, :], sem)


def _block_copy(src, dst_hbm, blk, sem):
    rows = src.shape[0]
    return pltpu.make_async_copy(src, dst_hbm.at[pl.ds(pl.multiple_of(blk * rows, rows), rows), :], sem)


def _dispatch_kernel(dest_ref, zflag_ref, u_ref, xg_hbm, zbuf, sem, zsem, *, n_tok, n_blocks):
    i = pl.program_id(0)
    tm = u_ref.shape[0] // TOKEN_ROWS

    @pl.when(i == 0)
    def _():
        zbuf[...] = jnp.zeros_like(zbuf)

        def zero_start(b, _):
            @pl.when(zflag_ref[b] != 0)
            def _():
                _block_copy(zbuf, xg_hbm, b, zsem).start()
            return 0

        def zero_wait(b, _):
            @pl.when(zflag_ref[b] != 0)
            def _():
                _block_copy(zbuf, xg_hbm, b, zsem).wait()
            return 0

        lax.fori_loop(0, n_blocks, zero_start, 0)
        lax.fori_loop(0, n_blocks, zero_wait, 0)

    for k in range(TOP_K):
        base = k * n_tok + i * tm

        def issue(it, _, base=base):
            r0 = it * DMA_UNROLL
            for j in range(DMA_UNROLL):
                _token_copy(u_ref, r0 + j, xg_hbm, dest_ref[base + r0 + j], sem).start(priority=j % 2)
            return 0

        lax.fori_loop(0, tm // DMA_UNROLL, issue, 0)

    def drain(it, _):
        for _j in range(DMA_UNROLL):
            _token_copy(u_ref, 0, xg_hbm, 0, sem).wait()
        return 0

    lax.fori_loop(0, TOP_K * tm // DMA_UNROLL, drain, 0)


def _dispatch(dest_flat, zflag, u_packed, n_blocks):
    t = u_packed.shape[0] // TOKEN_ROWS
    tm = DISPATCH_TM
    grid_spec = pltpu.PrefetchScalarGridSpec(
        num_scalar_prefetch=2,
        grid=(t // tm,),
        in_specs=[pl.BlockSpec((tm * TOKEN_ROWS, LANES), lambda i, d, z: (i, 0))],
        out_specs=pl.BlockSpec(memory_space=pl.ANY),
        scratch_shapes=[
            pltpu.VMEM((EXPERT_BLOCK * TOKEN_ROWS, LANES), U32),
            pltpu.SemaphoreType.DMA(()),
            pltpu.SemaphoreType.DMA(()),
        ],
    )
    return pl.pallas_call(
        functools.partial(_dispatch_kernel, n_tok=t, n_blocks=n_blocks),
        out_shape=jax.ShapeDtypeStruct((n_blocks * EXPERT_BLOCK * TOKEN_ROWS, LANES), U32),
        grid_spec=grid_spec,
        compiler_params=_params("arbitrary"),
        name="dispatch",
    )(dest_flat, zflag, u_packed)


def _silu_gate(g, up):
    return (g * (1.0 / (1.0 + jnp.exp(-g))) * up).astype(BF16)


def _expert_kernel(be_ref, nused_ref, x_ref, wg_ref, wu_ref, wd_ref, o_ref, wgu_bf, wd_bf):
    b = pl.program_id(0)
    half = D_MODEL // 2

    @pl.when(b < nused_ref[0])
    def _():
        @pl.when(jnp.logical_or(b == 0, be_ref[b] != be_ref[jnp.maximum(b - 1, 0)]))
        def _():
            wgu_bf[:, :D_EXPERT] = wg_ref[...].astype(BF16)
            wgu_bf[:, D_EXPERT:] = wu_ref[...].astype(BF16)
            wd_bf[...] = wd_ref[...].astype(BF16)

        lo, hi = _unpack_halves(_load_token_tiles(x_ref, EXPERT_BLOCK))
        gu = (jnp.dot(lo.astype(BF16), wgu_bf[:half, :], preferred_element_type=F32)
              + jnp.dot(hi.astype(BF16), wgu_bf[half:, :], preferred_element_type=F32))
        h = _silu_gate(gu[:, :D_EXPERT], gu[:, D_EXPERT:])
        out = jnp.dot(h, wd_bf[...], preferred_element_type=F32)
        _store_token_tiles(o_ref, _pack_halves(out.astype(BF16)))

    @pl.when(b >= nused_ref[0])
    def _():
        o_ref[...] = jnp.zeros_like(o_ref)


def _experts(blk_expert, n_used, xg, w_gate, w_up, w_down, layer, n_blocks):
    wspec = lambda r, c: pl.BlockSpec((None, None, r, c), lambda b, be, nu: (layer, be[b], 0, 0))
    blk_rows = EXPERT_BLOCK * TOKEN_ROWS
    grid_spec = pltpu.PrefetchScalarGridSpec(
        num_scalar_prefetch=2,
        grid=(n_blocks,),
        in_specs=[
            pl.BlockSpec((blk_rows, LANES), lambda b, be, nu: (jnp.minimum(b, nu[0] - 1), 0)),
            wspec(D_MODEL, D_EXPERT), wspec(D_MODEL, D_EXPERT), wspec(D_EXPERT, D_MODEL),
        ],
        out_specs=pl.BlockSpec((blk_rows, LANES), lambda b, be, nu: (b, 0)),
        scratch_shapes=[
            pltpu.VMEM((D_MODEL, 2 * D_EXPERT), BF16),
            pltpu.VMEM((D_EXPERT, D_MODEL), BF16),
        ],
    )
    return pl.pallas_call(
        _expert_kernel,
        out_shape=jax.ShapeDtypeStruct((n_blocks * blk_rows, LANES), U32),
        grid_spec=grid_spec,
        compiler_params=_params("arbitrary"),
        name="experts",
    )(blk_expert, n_used, xg, w_gate, w_up, w_down)


def _combine_kernel(dest_ref, eo_hbm, wts_ref, x_ref, sc_ref, sh_ref, g2_ref,
                    sg_ref, su_ref, sd_ref, lg_ref, lb_ref, o_ref, gbuf, sem, *, n_tok):
    i = pl.program_id(0)
    tm = x_ref.shape[0]
    half = D_MODEL // 2

    for k in range(TOP_K):
        base = k * n_tok + i * tm

        def issue(it, _, k=k, base=base):
            r0 = it * DMA_UNROLL
            for j in range(DMA_UNROLL):
                _token_copy(eo_hbm, dest_ref[base + r0 + j], gbuf.at[k], r0 + j, sem).start(priority=j % 2)
            return 0

        lax.fori_loop(0, tm // DMA_UNROLL, issue, 0)

    x = x_ref[...]
    u = (x * (1.0 + sc_ref[...]) + sh_ref[...]).astype(BF16)
    h = _silu_gate(jnp.dot(u, sg_ref[...], preferred_element_type=F32),
                   jnp.dot(u, su_ref[...], preferred_element_type=F32))
    shared = jnp.dot(h, sd_ref[...], preferred_element_type=F32)

    def drain(it, _):
        for _j in range(DMA_UNROLL):
            _token_copy(eo_hbm, 0, gbuf.at[0], 0, sem).wait()
        return 0

    lax.fori_loop(0, TOP_K * tm // DMA_UNROLL, drain, 0)

    wts = wts_ref[...]
    acc_lo, acc_hi = shared[:, :half], shared[:, half:]
    for k in range(TOP_K):
        lo, hi = _unpack_halves(_load_token_tiles(gbuf.at[k], tm))
        acc_lo = acc_lo + wts[:, k:k + 1] * lo
        acc_hi = acc_hi + wts[:, k:k + 1] * hi
    ffn = jnp.concatenate([acc_lo, acc_hi], axis=1)
    r = DEEPNORM_ALPHA * x + (1.0 + g2_ref[...]) * ffn
    o_ref[...] = _layer_norm(r, lg_ref[...], lb_ref[...])


def _combine(dest_flat, eo, wts_tk, x, sc, sh, g2, sg_bf, su_bf, sd_bf, ln_g, ln_b, layer):
    t = x.shape[0]
    tm = COMBINE_TM
    vec = lambda n: pl.BlockSpec((None, 1, n), lambda i, d: (layer, 0, 0))
    row = lambda: pl.BlockSpec((1, D_MODEL), lambda i, d: (0, 0))
    grid_spec = pltpu.PrefetchScalarGridSpec(
        num_scalar_prefetch=1,
        grid=(t // tm,),
        in_specs=[
            pl.BlockSpec(memory_space=pl.ANY),
            pl.BlockSpec((tm, TOP_K), lambda i, d: (i, 0)),
            pl.BlockSpec((tm, D_MODEL), lambda i, d: (i, 0)),
            row(), row(), row(),
            pl.BlockSpec((None, D_MODEL, D_EXPERT), lambda i, d: (layer, 0, 0)),
            pl.BlockSpec((None, D_MODEL, D_EXPERT), lambda i, d: (layer, 0, 0)),
            pl.BlockSpec((None, D_EXPERT, D_MODEL), lambda i, d: (layer, 0, 0)),
            vec(D_MODEL), vec(D_MODEL),
        ],
        out_specs=pl.BlockSpec((tm, D_MODEL), lambda i, d: (i, 0)),
        scratch_shapes=[
            pltpu.VMEM((TOP_K, tm * TOKEN_ROWS, LANES), U32),
            pltpu.SemaphoreType.DMA(()),
        ],
    )
    return pl.pallas_call(
        functools.partial(_combine_kernel, n_tok=t),
        out_shape=jax.ShapeDtypeStruct((t, D_MODEL), F32),
        grid_spec=grid_spec,
        compiler_params=_params("arbitrary"),
        name="combine_ln",
    )(dest_flat, eo, wts_tk, x, sc, sh, g2, sg_bf, su_bf, sd_bf,
      ln_g.reshape(DEPTH, 1, D_MODEL), ln_b.reshape(DEPTH, 1, D_MODEL))


def _block_plan(counts, n_blocks):
    blocks_per_e = (counts + EXPERT_BLOCK - 1) // EXPERT_BLOCK
    blk_end = jnp.cumsum(blocks_per_e)
    pad_start = ((blk_end - blocks_per_e) * EXPERT_BLOCK).astype(I32)
    n_used = blk_end[-1:].astype(I32)
    blk = jnp.arange(n_blocks, dtype=I32)[:, None]
    blk_expert = jnp.minimum(jnp.sum((blk_end[None, :] <= blk).astype(I32), axis=1), N_EXPERTS - 1)
    partial_blk = jnp.any((blk == blk_end[None, :] - 1) & (blocks_per_e[None, :] > 0), axis=1)
    zflag = (partial_blk | (blk[:, 0] >= n_used[0])).astype(I32)
    return pad_start, blk_expert.astype(I32), n_used, zflag


def kernel(x, c, w_mod, b_mod, w_in, conv_w, g_conv, g_att, w_out, ln1_g, ln1_b, w_router,
           router_bias, w_gate, w_up, w_down, ws_gate, ws_up, ws_down, ln2_g, ln2_b):
    bsz, seq, d = x.shape
    assert bsz == 1 and d == D_MODEL
    t = bsz * seq
    n_blocks = -(-(t * TOP_K) // EXPERT_BLOCK) + N_EXPERTS
    xt = x.reshape(t, d)

    mod = _modulation(c, w_mod, b_mod)
    w_in_bf = w_in.astype(BF16)
    w_out_bf = w_out.astype(BF16)
    sg_bf, su_bf, sd_bf = ws_gate.astype(BF16), ws_up.astype(BF16), ws_down.astype(BF16)
    w_router_t = jnp.swapaxes(w_router, 1, 2)

    for l in range(DEPTH):
        sh1, sc1, g1, sh2, sc2, g2 = [mod[l, :, n * d:(n + 1) * d] for n in range(N_MOD)]
        hbc = _inproj(xt, sc1, sh1, w_in_bf, l, 0, F32, 0)
        qkv = _inproj(xt, sc1, sh1, w_in_bf, l, 3 * D_CONV, BF16, D_ATT // INPROJ_TN)
        yc = _conv(hbc, conv_w, g_conv, l)
        att = _attention(qkv)
        xt = _outproj(yc, att, g_att, w_out_bf, xt, g1, ln1_g, ln1_b, l)
        u_packed, eidx, wts, rank, cnt = _router(xt, sc2, sh2, w_router_t, router_bias, l)
        pad_start, blk_expert, n_used, zflag = _block_plan(cnt[:, 0].astype(I32), n_blocks)
        dest = _dest_rows(pad_start, eidx, rank).reshape(-1)
        xg = _dispatch(dest, zflag, u_packed, n_blocks)
        eo = _experts(blk_expert, n_used, xg, w_gate, w_up, w_down, l, n_blocks)
        xt = _combine(dest, eo, wts.T, xt, sc2, sh2, g2, sg_bf, su_bf, sd_bf, ln2_g, ln2_b, l)
    return xt.reshape(bsz, seq, d)
```

```python
import functools

import jax
import jax.numpy as jnp
from jax import lax
from jax.experimental import pallas as pl
from jax.experimental.pallas import tpu as pltpu

F32 = jnp.float32
BF16 = jnp.bfloat16
I32 = jnp.int32
U32 = jnp.uint32

D_MODEL = 2048
DEPTH = 4
D_CONV = D_MODEL // 2
D_ATT = D_MODEL // 2
HEAD_DIM = 128
N_HEADS = D_ATT // HEAD_DIM
N_EXPERTS = 64
TOP_K = 8
N_GROUPS = 8
GROUP_SIZE = N_EXPERTS // N_GROUPS
TOPK_GROUPS = 4
D_EXPERT = 384
ROUTED_SCALE = 2.5
EXPERT_BLOCK = 256
EXPERT_CHUNK = 128
DEEPNORM_ALPHA = (2 * DEPTH) ** 0.25
LN_EPS = 1e-5
RMS_EPS = 1e-6
N_MOD = 6
ATT_SCALE = HEAD_DIM ** -0.5
EXP_UNDERFLOW = -110.0

LANES = 128
SUBLANES = 8
VMEM_LIMIT = 48 * 1024 * 1024

MOD_TN = 1024
INPROJ_TM = 512
INPROJ_TN = 1024
CONV_TM = 512
ATT_BQ = 256
ATT_BK = 256
OUT_TM = 256
ROUTER_TM = 512
COMBINE_TM = 256
DISPATCH_TM = 256
DMA_UNROLL = 8
TOKEN_ROWS = (D_MODEL // 2) // LANES


def _params(*sem):
    return pltpu.CompilerParams(dimension_semantics=sem, vmem_limit_bytes=VMEM_LIMIT)


def _mod_kernel(c_ref, w_ref, b_ref, o_ref):
    o_ref[...] = jnp.dot(c_ref[...], w_ref[...], preferred_element_type=F32) + b_ref[...]


def _modulation(c, w_mod, b_mod):
    n = N_MOD * D_MODEL
    c8 = jnp.broadcast_to(c, (SUBLANES, D_MODEL))
    out = pl.pallas_call(
        _mod_kernel,
        out_shape=jax.ShapeDtypeStruct((DEPTH, SUBLANES, n), F32),
        grid=(DEPTH, n // MOD_TN),
        in_specs=[
            pl.BlockSpec((SUBLANES, D_MODEL), lambda l, j: (0, 0)),
            pl.BlockSpec((None, D_MODEL, MOD_TN), lambda l, j: (l, 0, j)),
            pl.BlockSpec((None, 1, MOD_TN), lambda l, j: (l, 0, j)),
        ],
        out_specs=pl.BlockSpec((None, SUBLANES, MOD_TN), lambda l, j: (l, 0, j)),
        compiler_params=_params("arbitrary", "arbitrary"),
        name="modulation",
    )(c8, w_mod, b_mod.reshape(DEPTH, 1, n))
    return out[:, 0:1, :]


def _inproj_kernel(x_ref, sc_ref, sh_ref, w_ref, o_ref, *, q_tiles):
    u = (x_ref[...] * (1.0 + sc_ref[...]) + sh_ref[...]).astype(BF16)
    acc = jnp.dot(u, w_ref[...], preferred_element_type=F32)
    if q_tiles:
        acc = acc * jnp.where(pl.program_id(0) < q_tiles, ATT_SCALE, 1.0)
    o_ref[...] = acc.astype(o_ref.dtype)


def _inproj(x, sc, sh, w_bf, layer, col0, out_dtype, q_tiles):
    t = x.shape[0]
    n = 3 * D_CONV
    tile0 = col0 // INPROJ_TN
    return pl.pallas_call(
        functools.partial(_inproj_kernel, q_tiles=q_tiles),
        out_shape=jax.ShapeDtypeStruct((t, n), out_dtype),
        grid=(n // INPROJ_TN, t // INPROJ_TM),
        in_specs=[
            pl.BlockSpec((INPROJ_TM, D_MODEL), lambda j, i: (i, 0)),
            pl.BlockSpec((1, D_MODEL), lambda j, i: (0, 0)),
            pl.BlockSpec((1, D_MODEL), lambda j, i: (0, 0)),
            pl.BlockSpec((None, D_MODEL, INPROJ_TN), lambda j, i: (layer, 0, tile0 + j)),
        ],
        out_specs=pl.BlockSpec((INPROJ_TM, INPROJ_TN), lambda j, i: (i, j)),
        compiler_params=_params("arbitrary", "arbitrary"),
        name="inproj",
    )(x, sc, sh, w_bf)


def _conv_kernel(h_ref, b_ref, c_ref, hh_ref, ch_ref, cw_ref, g_ref, o_ref):
    i = pl.program_id(0)
    v = c_ref[...] * h_ref[...]
    halo = jnp.where(i > 0, ch_ref[...] * hh_ref[...], 0.0)
    row = lax.broadcasted_iota(I32, v.shape, 0)
    prev1 = halo[SUBLANES - 1:SUBLANES, :]
    prev2 = halo[SUBLANES - 2:SUBLANES - 1, :]
    v1 = jnp.where(row == 0, prev1, pltpu.roll(v, 1, axis=0))
    v2 = jnp.where(row == 0, prev2, jnp.where(row == 1, prev1, pltpu.roll(v, 2, axis=0)))
    cw = cw_ref[...]
    conv = cw[0:1, :] * v2 + cw[1:2, :] * v1 + cw[2:3, :] * v
    y = b_ref[...] * conv
    ms = jnp.mean(y * y, axis=-1, keepdims=True)
    o_ref[...] = (y * lax.rsqrt(ms + RMS_EPS) * g_ref[...]).astype(o_ref.dtype)


def _conv(hbc, conv_w, g_conv, layer):
    t = hbc.shape[0]
    halo_blocks = CONV_TM // SUBLANES

    def halo_map(col):
        return lambda i: (jnp.maximum(i * halo_blocks - 1, 0), col)

    return pl.pallas_call(
        _conv_kernel,
        out_shape=jax.ShapeDtypeStruct((t, D_CONV), BF16),
        grid=(t // CONV_TM,),
        in_specs=[
            pl.BlockSpec((CONV_TM, D_CONV), lambda i: (i, 0)),
            pl.BlockSpec((CONV_TM, D_CONV), lambda i: (i, 1)),
            pl.BlockSpec((CONV_TM, D_CONV), lambda i: (i, 2)),
            pl.BlockSpec((SUBLANES, D_CONV), halo_map(0)),
            pl.BlockSpec((SUBLANES, D_CONV), halo_map(2)),
            pl.BlockSpec((None, 3, D_CONV), lambda i: (layer, 0, 0)),
            pl.BlockSpec((None, 1, D_CONV), lambda i: (layer, 0, 0)),
        ],
        out_specs=pl.BlockSpec((CONV_TM, D_CONV), lambda i: (i, 0)),
        compiler_params=_params("arbitrary"),
        name="conv_rms",
    )(hbc, hbc, hbc, hbc, hbc, conv_w, g_conv.reshape(DEPTH, 1, D_CONV))


def _attn_kernel(q_ref, k_ref, v_ref, o_ref, *, bq, bk):
    i = pl.program_id(1)
    q = q_ref[...]
    row = lax.broadcasted_iota(I32, (bq, bk), 0)
    col = lax.broadcasted_iota(I32, (bq, bk), 1)
    causal = col < row
    r2 = lax.broadcasted_iota(I32, (bk, bk + LANES), 0)
    c2 = lax.broadcasted_iota(I32, (bk, bk + LANES), 1)
    tri = jnp.where((r2 >= c2) | (c2 >= bk), 1.0, 0.0).astype(BF16)

    def block(j, later, diagonal):
        start = pl.multiple_of(j * bk, bk)
        ks = k_ref[pl.ds(start, bk), :]
        vs = v_ref[pl.ds(start, bk), :]
        z = lax.dot_general(q, ks, (((1,), (1,)), ((), ())), preferred_element_type=F32)
        log_keep = -(jnp.maximum(z, 0.0) + jnp.log(1.0 + jnp.exp(-jnp.abs(z))))
        if diagonal:
            log_keep = jnp.where(causal, log_keep, 0.0)
        hi = log_keep.astype(BF16)
        lo = (log_keep - hi.astype(F32)).astype(BF16)
        sums = (jnp.dot(hi, tri, preferred_element_type=F32)
                + jnp.dot(lo, tri, preferred_element_type=F32))
        incl = sums[:, :bk] + jnp.concatenate([later] * (bk // LANES), axis=1)
        a = jnp.exp(z + incl)
        if diagonal:
            a = jnp.where(causal, a, 0.0)
        pv = jnp.dot(a.astype(BF16), vs, preferred_element_type=F32)
        return pv, later + sums[:, bk:]

    acc, later = block(i, jnp.zeros((bq, LANES), F32), True)

    def cond(c):
        jj, _, _, live = c
        return jnp.logical_and(jj <= i, live)

    def body(c):
        jj, acc, later, _ = c
        pv, later = block(i - jj, later, False)
        return jj + 1, acc + pv, later, jnp.max(later) > EXP_UNDERFLOW

    _, acc, _, _ = lax.while_loop(cond, body, (jnp.int32(1), acc, later, jnp.max(later) > EXP_UNDERFLOW))
    o_ref[...] = acc


def _attention(qkv):
    t = qkv.shape[0]
    bq, bk = ATT_BQ, ATT_BK
    assert bq == bk and HEAD_DIM == LANES
    return pl.pallas_call(
        functools.partial(_attn_kernel, bq=bq, bk=bk),
        out_shape=jax.ShapeDtypeStruct((t, D_ATT), F32),
        grid=(N_HEADS, t // bq),
        in_specs=[
            pl.BlockSpec((bq, HEAD_DIM), lambda h, i: (i, h)),
            pl.BlockSpec((t, HEAD_DIM), lambda h, i: (0, N_HEADS + h)),
            pl.BlockSpec((t, HEAD_DIM), lambda h, i: (0, 2 * N_HEADS + h)),
        ],
        out_specs=pl.BlockSpec((bq, HEAD_DIM), lambda h, i: (i, h)),
        compiler_params=_params("arbitrary", "arbitrary"),
        name="stickbreak_attn",
    )(qkv, qkv, qkv)


def _layer_norm(r, g, b):
    mu = jnp.mean(r, axis=-1, keepdims=True)
    rc = r - mu
    var = jnp.mean(rc * rc, axis=-1, keepdims=True)
    return rc * lax.rsqrt(var + LN_EPS) * g + b


def _outproj_kernel(yc_ref, att_ref, ga_ref, wc_ref, wa_ref, x_ref, g1_ref, lg_ref, lb_ref, o_ref):
    att = att_ref[...]
    ms = jnp.mean(att * att, axis=-1, keepdims=True)
    ya = (att * lax.rsqrt(ms + RMS_EPS) * ga_ref[...]).astype(BF16)
    mix = (jnp.dot(yc_ref[...], wc_ref[...], preferred_element_type=F32)
           + jnp.dot(ya, wa_ref[...], preferred_element_type=F32))
    r = DEEPNORM_ALPHA * x_ref[...] + (1.0 + g1_ref[...]) * mix
    o_ref[...] = _layer_norm(r, lg_ref[...], lb_ref[...])


def _outproj(yc, att, g_att, w_out_bf, x, g1, ln_g, ln_b, layer):
    t = x.shape[0]
    vec = lambda n: pl.BlockSpec((None, 1, n), lambda i: (layer, 0, 0))
    return pl.pallas_call(
        _outproj_kernel,
        out_shape=jax.ShapeDtypeStruct((t, D_MODEL), F32),
        grid=(t // OUT_TM,),
        in_specs=[
            pl.BlockSpec((OUT_TM, D_CONV), lambda i: (i, 0)),
            pl.BlockSpec((OUT_TM, D_ATT), lambda i: (i, 0)),
            vec(D_ATT),
            pl.BlockSpec((None, D_CONV, D_MODEL), lambda i: (layer, 0, 0)),
            pl.BlockSpec((None, D_ATT, D_MODEL), lambda i: (layer, 1, 0)),
            pl.BlockSpec((OUT_TM, D_MODEL), lambda i: (i, 0)),
            pl.BlockSpec((1, D_MODEL), lambda i: (0, 0)),
            vec(D_MODEL),
            vec(D_MODEL),
        ],
        out_specs=pl.BlockSpec((OUT_TM, D_MODEL), lambda i: (i, 0)),
        compiler_params=_params("arbitrary"),
        name="outproj_ln",
    )(yc, att, g_att.reshape(DEPTH, 1, D_ATT), w_out_bf, w_out_bf, x, g1,
      ln_g.reshape(DEPTH, 1, D_MODEL), ln_b.reshape(DEPTH, 1, D_MODEL))


def _split_bf16(a):
    hi = a.astype(BF16)
    lo = (a - hi.astype(F32)).astype(BF16)
    return hi, lo


def _pack_halves(a_bf):
    n = a_bf.shape[1] // 2
    bits = lax.bitcast_convert_type(a_bf.astype(F32), U32)
    return (bits[:, :n] >> 16) | (bits[:, n:] & jnp.uint32(0xFFFF0000))


def _unpack_halves(words):
    lo = lax.bitcast_convert_type(words << 16, F32)
    hi = lax.bitcast_convert_type(words & jnp.uint32(0xFFFF0000), F32)
    return lo, hi


def _store_token_tiles(ref, words, tok0=0):
    m = words.shape[0]
    for g in range(TOKEN_ROWS):
        ref[pl.ds(tok0 * TOKEN_ROWS + g, m, stride=TOKEN_ROWS), :] = words[:, g * LANES:(g + 1) * LANES]


def _load_token_tiles(ref, m, tok0=0):
    return jnp.concatenate(
        [ref[pl.ds(tok0 * TOKEN_ROWS + g, m, stride=TOKEN_ROWS), :] for g in range(TOKEN_ROWS)], axis=1)


def _router_kernel(x_ref, sc_ref, sh_ref, wr_ref, bias_ref,
                   u_ref, eidx_ref, wts_ref, rank_ref, cnt_ref):
    step = pl.program_id(0)
    tm = x_ref.shape[0]
    neg_inf = -jnp.inf

    @pl.when(step == 0)
    def _():
        cnt_ref[...] = jnp.zeros_like(cnt_ref)

    u = x_ref[...] * (1.0 + sc_ref[...]) + sh_ref[...]
    u_hi, u_lo = _split_bf16(u)
    _store_token_tiles(u_ref, _pack_halves(u_hi))
    w_hi, w_lo = _split_bf16(wr_ref[...])
    nt = (((1,), (1,)), ((), ()))
    logits = (lax.dot_general(w_hi, u_hi, nt, preferred_element_type=F32)
              + lax.dot_general(w_hi, u_lo, nt, preferred_element_type=F32)
              + lax.dot_general(w_lo, u_hi, nt, preferred_element_type=F32))
    scores = 1.0 / (1.0 + jnp.exp(-logits))
    sel = scores + bias_ref[...]

    gscore = []
    for g in range(N_GROUPS):
        grp = sel[g * GROUP_SIZE:(g + 1) * GROUP_SIZE, :]
        m1 = jnp.max(grp, axis=0, keepdims=True)
        is_max = grp == m1
        n_max = jnp.sum(is_max.astype(F32), axis=0, keepdims=True)
        rest = jnp.max(jnp.where(is_max, neg_inf, grp), axis=0, keepdims=True)
        gscore.append(m1 + jnp.where(n_max >= 2.0, m1, rest))
    masked = []
    for g in range(N_GROUPS):
        beaten = jnp.zeros((1, tm), F32)
        for o in range(N_GROUPS):
            if o == g:
                continue
            wins = (gscore[o] >= gscore[g]) if o < g else (gscore[o] > gscore[g])
            beaten = beaten + wins.astype(F32)
        keep = beaten < float(TOPK_GROUPS)
        grp = sel[g * GROUP_SIZE:(g + 1) * GROUP_SIZE, :]
        masked.append(jnp.where(keep, grp, neg_inf))
    cand = jnp.concatenate(masked, axis=0)

    eiota = lax.broadcasted_iota(I32, (N_EXPERTS, tm), 0)
    picked = jnp.zeros((N_EXPERTS, tm), F32)
    idxs, ws = [], []
    for _ in range(TOP_K):
        m = jnp.max(cand, axis=0, keepdims=True)
        idx = jnp.min(jnp.where(cand == m, eiota, N_EXPERTS), axis=0, keepdims=True)
        hit = eiota == idx
        ws.append(jnp.sum(jnp.where(hit, scores, 0.0), axis=0, keepdims=True))
        idxs.append(idx)
        picked = jnp.where(hit, 1.0, picked)
        cand = jnp.where(hit, neg_inf, cand)
    wsum = ws[0]
    for k in range(1, TOP_K):
        wsum = wsum + ws[k]
    norm = ROUTED_SCALE / wsum

    r = lax.broadcasted_iota(I32, (tm, tm), 0)
    c = lax.broadcasted_iota(I32, (tm, tm), 1)
    before = jnp.where(r < c, 1.0, 0.0).astype(BF16)
    ones = jnp.ones((tm, LANES), BF16)
    picked_bf = picked.astype(BF16)
    prior = jnp.dot(picked_bf, before, preferred_element_type=F32)
    total = jnp.dot(picked_bf, ones, preferred_element_type=F32)
    carried = cnt_ref[...]
    rank_full = prior + jnp.concatenate([carried] * (tm // LANES), axis=1)
    cnt_ref[...] = carried + total

    for k in range(TOP_K):
        hit = eiota == idxs[k]
        eidx_ref[k:k + 1, :] = idxs[k]
        wts_ref[k:k + 1, :] = ws[k] * norm
        rank_ref[k:k + 1, :] = jnp.sum(jnp.where(hit, rank_full, 0.0), axis=0,
                                       keepdims=True).astype(I32)


def _router(x, sc, sh, w_router_t, router_bias, layer):
    t = x.shape[0]
    tm = ROUTER_TM
    return pl.pallas_call(
        _router_kernel,
        out_shape=(
            jax.ShapeDtypeStruct((t * TOKEN_ROWS, LANES), U32),
            jax.ShapeDtypeStruct((TOP_K, t), I32),
            jax.ShapeDtypeStruct((TOP_K, t), F32),
            jax.ShapeDtypeStruct((TOP_K, t), I32),
            jax.ShapeDtypeStruct((N_EXPERTS, LANES), F32),
        ),
        grid=(t // tm,),
        in_specs=[
            pl.BlockSpec((tm, D_MODEL), lambda i: (i, 0)),
            pl.BlockSpec((1, D_MODEL), lambda i: (0, 0)),
            pl.BlockSpec((1, D_MODEL), lambda i: (0, 0)),
            pl.BlockSpec((None, N_EXPERTS, D_MODEL), lambda i: (layer, 0, 0)),
            pl.BlockSpec((None, N_EXPERTS, 1), lambda i: (layer, 0, 0)),
        ],
        out_specs=(
            pl.BlockSpec((tm * TOKEN_ROWS, LANES), lambda i: (i, 0)),
            pl.BlockSpec((TOP_K, tm), lambda i: (0, i)),
            pl.BlockSpec((TOP_K, tm), lambda i: (0, i)),
            pl.BlockSpec((TOP_K, tm), lambda i: (0, i)),
            pl.BlockSpec((N_EXPERTS, LANES), lambda i: (0, 0)),
        ),
        compiler_params=_params("arbitrary"),
        name="router",
    )(x, sc, sh, w_router_t, router_bias.reshape(DEPTH, N_EXPERTS, 1))


def _dest_kernel(pad_start_ref, eidx_ref, rank_ref, dest_ref):
    eidx = eidx_ref[...]
    dest = rank_ref[...]
    for e in range(N_EXPERTS):
        dest = dest + jnp.where(eidx == e, pad_start_ref[e], 0)
    dest_ref[...] = dest


def _dest_rows(pad_start, eidx, rank):
    k, t = eidx.shape
    grid_spec = pltpu.PrefetchScalarGridSpec(
        num_scalar_prefetch=1,
        grid=(1,),
        in_specs=[pl.BlockSpec((k, t), lambda i, p: (0, 0)), pl.BlockSpec((k, t), lambda i, p: (0, 0))],
        out_specs=pl.BlockSpec((k, t), lambda i, p: (0, 0)),
    )
    return pl.pallas_call(
        _dest_kernel,
        out_shape=jax.ShapeDtypeStruct((k, t), I32),
        grid_spec=grid_spec,
        compiler_params=_params("arbitrary"),
        name="dest_rows",
    )(pad_start, eidx, rank)


def _token_copy(src, src_tok, dst, dst_tok, sem):
    s = pl.ds(pl.multiple_of(src_tok * TOKEN_ROWS, TOKEN_ROWS), TOKEN_ROWS)
    d = pl.ds(pl.multiple_of(dst_tok * TOKEN_ROWS, TOKEN_ROWS), TOKEN_ROWS)
    return pltpu.make_async_copy(src.at[s, :], dst.at[d, :], sem)


def _block_copy(src, dst_hbm, blk, sem):
    rows = src.shape[0]
    return pltpu.make_async_copy(src, dst_hbm.at[pl.ds(pl.multiple_of(blk * rows, rows), rows), :], sem)


def _dispatch_kernel(dest_ref, zflag_ref, u_ref, xg_hbm, zbuf, sem, zsem, *, n_tok, n_blocks):
    i = pl.program_id(0)
    tm = u_ref.shape[0] // TOKEN_ROWS

    @pl.when(i == 0)
    def _():
        zbuf[...] = jnp.zeros_like(zbuf)

        def zero_start(b, _):
            @pl.when(zflag_ref[b] != 0)
            def _():
                _block_copy(zbuf, xg_hbm, b, zsem).start()
            return 0

        def zero_wait(b, _):
            @pl.when(zflag_ref[b] != 0)
            def _():
                _block_copy(zbuf, xg_hbm, b, zsem).wait()
            return 0

        lax.fori_loop(0, n_blocks, zero_start, 0)
        lax.fori_loop(0, n_blocks, zero_wait, 0)

    for k in range(TOP_K):
        base = k * n_tok + i * tm

        def issue(it, _, base=base):
            r0 = it * DMA_UNROLL
            for j in range(DMA_UNROLL):
                _token_copy(u_ref, r0 + j, xg_hbm, dest_ref[base + r0 + j], sem).start(priority=j % 2)
            return 0

        lax.fori_loop(0, tm // DMA_UNROLL, issue, 0)

    def drain(it, _):
        for _j in range(DMA_UNROLL):
            _token_copy(u_ref, 0, xg_hbm, 0, sem).wait()
        return 0

    lax.fori_loop(0, TOP_K * tm // DMA_UNROLL, drain, 0)


def _dispatch(dest_flat, zflag, u_packed, n_blocks):
    t = u_packed.shape[0] // TOKEN_ROWS
    tm = DISPATCH_TM
    grid_spec = pltpu.PrefetchScalarGridSpec(
        num_scalar_prefetch=2,
        grid=(t // tm,),
        in_specs=[pl.BlockSpec((tm * TOKEN_ROWS, LANES), lambda i, d, z: (i, 0))],
        out_specs=pl.BlockSpec(memory_space=pl.ANY),
        scratch_shapes=[
            pltpu.VMEM((EXPERT_BLOCK * TOKEN_ROWS, LANES), U32),
            pltpu.SemaphoreType.DMA(()),
            pltpu.SemaphoreType.DMA(()),
        ],
    )
    return pl.pallas_call(
        functools.partial(_dispatch_kernel, n_tok=t, n_blocks=n_blocks),
        out_shape=jax.ShapeDtypeStruct((n_blocks * EXPERT_BLOCK * TOKEN_ROWS, LANES), U32),
        grid_spec=grid_spec,
        compiler_params=_params("arbitrary"),
        name="dispatch",
    )(dest_flat, zflag, u_packed)


def _silu_gate(g, up):
    return (g * (1.0 / (1.0 + jnp.exp(-g))) * up).astype(BF16)


def _expert_kernel(be_ref, nused_ref, nxt_ref, slot_ref, x_ref, wg_hbm, wu_hbm, wd_hbm, o_ref,
                   wg_buf, wu_buf, wd_buf, wgu_bf, wd_bf, sems, *, layer):
    b = pl.program_id(0)

    def weight_copies(expert, slot):
        return (pltpu.make_async_copy(wg_hbm.at[layer, expert], wg_buf.at[slot], sems.at[slot]),
                pltpu.make_async_copy(wu_hbm.at[layer, expert], wu_buf.at[slot], sems.at[slot]),
                pltpu.make_async_copy(wd_hbm.at[layer, expert], wd_buf.at[slot], sems.at[slot]))

    @pl.when(b < nused_ref[0])
    def _():
        expert = be_ref[b]
        slot = slot_ref[b]

        @pl.when(b == 0)
        def _():
            for cp in weight_copies(expert, slot):
                cp.start()

        @pl.when(jnp.logical_or(b == 0, expert != be_ref[jnp.maximum(b - 1, 0)]))
        def _():
            nxt = nxt_ref[b]

            @pl.when(nxt >= 0)
            def _():
                for cp in weight_copies(nxt, 1 - slot):
                    cp.start()

            for cp in weight_copies(expert, slot):
                cp.wait()
            wgu_bf[:, :D_EXPERT] = wg_buf[slot].astype(BF16)
            wgu_bf[:, D_EXPERT:] = wu_buf[slot].astype(BF16)
            wd_bf[...] = wd_buf[slot].astype(BF16)

        for tok0 in range(0, EXPERT_BLOCK, EXPERT_CHUNK):
            lo, hi = _unpack_halves(_load_token_tiles(x_ref, EXPERT_CHUNK, tok0))
            x = jnp.concatenate([lo.astype(BF16), hi.astype(BF16)], axis=1)
            gu = jnp.dot(x, wgu_bf[...], preferred_element_type=F32)
            h = _silu_gate(gu[:, :D_EXPERT], gu[:, D_EXPERT:])
            out = jnp.dot(h, wd_bf[...], preferred_element_type=F32)
            _store_token_tiles(o_ref, _pack_halves(out.astype(BF16)), tok0)

    @pl.when(b >= nused_ref[0])
    def _():
        o_ref[...] = jnp.zeros_like(o_ref)


def _experts(blk_expert, n_used, blk_next, blk_slot, xg, w_gate, w_up, w_down, layer, n_blocks):
    blk_rows = EXPERT_BLOCK * TOKEN_ROWS
    hbm = lambda: pl.BlockSpec(memory_space=pl.ANY)
    grid_spec = pltpu.PrefetchScalarGridSpec(
        num_scalar_prefetch=4,
        grid=(n_blocks,),
        in_specs=[
            pl.BlockSpec((blk_rows, LANES), lambda b, be, nu, nx, sl: (jnp.minimum(b, nu[0] - 1), 0)),
            hbm(), hbm(), hbm(),
        ],
        out_specs=pl.BlockSpec((blk_rows, LANES), lambda b, be, nu, nx, sl: (b, 0)),
        scratch_shapes=[
            pltpu.VMEM((2, D_MODEL, D_EXPERT), F32),
            pltpu.VMEM((2, D_MODEL, D_EXPERT), F32),
            pltpu.VMEM((2, D_EXPERT, D_MODEL), F32),
            pltpu.VMEM((D_MODEL, 2 * D_EXPERT), BF16),
            pltpu.VMEM((D_EXPERT, D_MODEL), BF16),
            pltpu.SemaphoreType.DMA((2,)),
        ],
    )
    return pl.pallas_call(
        functools.partial(_expert_kernel, layer=layer),
        out_shape=jax.ShapeDtypeStruct((n_blocks * blk_rows, LANES), U32),
        grid_spec=grid_spec,
        compiler_params=_params("arbitrary"),
        name="experts",
    )(blk_expert, n_used, blk_next, blk_slot, xg, w_gate, w_up, w_down)


def _combine_kernel(dest_ref, eo_hbm, wts_ref, x_ref, sc_ref, sh_ref, g2_ref,
                    sg_ref, su_ref, sd_ref, lg_ref, lb_ref, o_ref, gbuf, sems, *, n_tok):
    i = pl.program_id(0)
    tm = x_ref.shape[0]
    half = D_MODEL // 2
    slot = i % 2

    def gather_tile(tile, into):
        for k in range(TOP_K):
            base = k * n_tok + tile * tm

            def issue(it, _, k=k, base=base):
                r0 = it * DMA_UNROLL
                for j in range(DMA_UNROLL):
                    _token_copy(eo_hbm, dest_ref[base + r0 + j], gbuf.at[into, k], r0 + j,
                                sems.at[into]).start(priority=j % 2)
                return 0

            lax.fori_loop(0, tm // DMA_UNROLL, issue, 0)

    @pl.when(i == 0)
    def _():
        gather_tile(0, 0)

    @pl.when(i + 1 < pl.num_programs(0))
    def _():
        gather_tile(i + 1, 1 - slot)

    x = x_ref[...]
    u = (x * (1.0 + sc_ref[...]) + sh_ref[...]).astype(BF16)
    h = _silu_gate(jnp.dot(u, sg_ref[...], preferred_element_type=F32),
                   jnp.dot(u, su_ref[...], preferred_element_type=F32))
    shared = jnp.dot(h, sd_ref[...], preferred_element_type=F32)

    def drain(it, _):
        for _j in range(DMA_UNROLL):
            _token_copy(eo_hbm, 0, gbuf.at[slot, 0], 0, sems.at[slot]).wait()
        return 0

    lax.fori_loop(0, TOP_K * tm // DMA_UNROLL, drain, 0)

    wts = wts_ref[...]
    acc_lo, acc_hi = shared[:, :half], shared[:, half:]
    for k in range(TOP_K):
        lo, hi = _unpack_halves(_load_token_tiles(gbuf.at[slot, k], tm))
        acc_lo = acc_lo + wts[:, k:k + 1] * lo
        acc_hi = acc_hi + wts[:, k:k + 1] * hi
    ffn = jnp.concatenate([acc_lo, acc_hi], axis=1)
    r = DEEPNORM_ALPHA * x + (1.0 + g2_ref[...]) * ffn
    o_ref[...] = _layer_norm(r, lg_ref[...], lb_ref[...])


def _combine(dest_flat, eo, wts_tk, x, sc, sh, g2, sg_bf, su_bf, sd_bf, ln_g, ln_b, layer):
    t = x.shape[0]
    tm = COMBINE_TM
    vec = lambda n: pl.BlockSpec((None, 1, n), lambda i, d: (layer, 0, 0))
    row = lambda: pl.BlockSpec((1, D_MODEL), lambda i, d: (0, 0))
    grid_spec = pltpu.PrefetchScalarGridSpec(
        num_scalar_prefetch=1,
        grid=(t // tm,),
        in_specs=[
            pl.BlockSpec(memory_space=pl.ANY),
            pl.BlockSpec((tm, TOP_K), lambda i, d: (i, 0)),
            pl.BlockSpec((tm, D_MODEL), lambda i, d: (i, 0)),
            row(), row(), row(),
            pl.BlockSpec((None, D_MODEL, D_EXPERT), lambda i, d: (layer, 0, 0)),
            pl.BlockSpec((None, D_MODEL, D_EXPERT), lambda i, d: (layer, 0, 0)),
            pl.BlockSpec((None, D_EXPERT, D_MODEL), lambda i, d: (layer, 0, 0)),
            vec(D_MODEL), vec(D_MODEL),
        ],
        out_specs=pl.BlockSpec((tm, D_MODEL), lambda i, d: (i, 0)),
        scratch_shapes=[
            pltpu.VMEM((2, TOP_K, tm * TOKEN_ROWS, LANES), U32),
            pltpu.SemaphoreType.DMA((2,)),
        ],
    )
    return pl.pallas_call(
        functools.partial(_combine_kernel, n_tok=t),
        out_shape=jax.ShapeDtypeStruct((t, D_MODEL), F32),
        grid_spec=grid_spec,
        compiler_params=_params("arbitrary"),
        name="combine_ln",
    )(dest_flat, eo, wts_tk, x, sc, sh, g2, sg_bf, su_bf, sd_bf,
      ln_g.reshape(DEPTH, 1, D_MODEL), ln_b.reshape(DEPTH, 1, D_MODEL))


def _block_plan(counts, n_blocks):
    blocks_per_e = (counts + EXPERT_BLOCK - 1) // EXPERT_BLOCK
    blk_end = jnp.cumsum(blocks_per_e)
    pad_start = ((blk_end - blocks_per_e) * EXPERT_BLOCK).astype(I32)
    n_used = blk_end[-1:].astype(I32)
    blk = jnp.arange(n_blocks, dtype=I32)[:, None]
    blk_expert = jnp.minimum(jnp.sum((blk_end[None, :] <= blk).astype(I32), axis=1), N_EXPERTS - 1)
    partial_blk = jnp.any((blk == blk_end[None, :] - 1) & (blocks_per_e[None, :] > 0), axis=1)
    zflag = (partial_blk | (blk[:, 0] >= n_used[0])).astype(I32)
    eid = jnp.arange(N_EXPERTS, dtype=I32)
    nonempty = blocks_per_e > 0
    parity = (jnp.cumsum(nonempty.astype(I32)) - 1) & 1
    later = nonempty[None, :] & (eid[None, :] > eid[:, None])
    next_e = jnp.min(jnp.where(later, eid[None, :], N_EXPERTS), axis=1)
    next_e = jnp.where(next_e == N_EXPERTS, -1, next_e)
    onehot = (blk_expert[:, None] == eid[None, :]).astype(I32)
    blk_slot = jnp.sum(onehot * parity[None, :], axis=1).astype(I32)
    blk_next = jnp.sum(onehot * next_e[None, :], axis=1).astype(I32)
    return pad_start, blk_expert.astype(I32), n_used, zflag, blk_next, blk_slot


def kernel(x, c, w_mod, b_mod, w_in, conv_w, g_conv, g_att, w_out, ln1_g, ln1_b, w_router,
           router_bias, w_gate, w_up, w_down, ws_gate, ws_up, ws_down, ln2_g, ln2_b):
    bsz, seq, d = x.shape
    assert bsz == 1 and d == D_MODEL
    t = bsz * seq
    n_blocks = -(-(t * TOP_K) // EXPERT_BLOCK) + N_EXPERTS
    xt = x.reshape(t, d)

    mod = _modulation(c, w_mod, b_mod)
    w_in_bf = w_in.astype(BF16)
    w_out_bf = w_out.astype(BF16)
    sg_bf, su_bf, sd_bf = ws_gate.astype(BF16), ws_up.astype(BF16), ws_down.astype(BF16)
    w_router_t = jnp.swapaxes(w_router, 1, 2)

    for l in range(DEPTH):
        sh1, sc1, g1, sh2, sc2, g2 = [mod[l, :, n * d:(n + 1) * d] for n in range(N_MOD)]
        hbc = _inproj(xt, sc1, sh1, w_in_bf, l, 0, F32, 0)
        qkv = _inproj(xt, sc1, sh1, w_in_bf, l, 3 * D_CONV, BF16, D_ATT // INPROJ_TN)
        yc = _conv(hbc, conv_w, g_conv, l)
        att = _attention(qkv)
        xt = _outproj(yc, att, g_att, w_out_bf, xt, g1, ln1_g, ln1_b, l)
        u_packed, eidx, wts, rank, cnt = _router(xt, sc2, sh2, w_router_t, router_bias, l)
        pad_start, blk_expert, n_used, zflag, blk_next, blk_slot = _block_plan(
            cnt[:, 0].astype(I32), n_blocks)
        dest = _dest_rows(pad_start, eidx, rank).reshape(-1)
        xg = _dispatch(dest, zflag, u_packed, n_blocks)
        eo = _experts(blk_expert, n_used, blk_next, blk_slot, xg, w_gate, w_up, w_down, l, n_blocks)
        xt = _combine(dest, eo, wts.T, xt, sc2, sh2, g2, sg_bf, su_bf, sd_bf, ln2_g, ln2_b, l)
    return xt.reshape(bsz, seq, d)
```

```python
import functools

import jax
import jax.numpy as jnp
from jax import lax
from jax.experimental import pallas as pl
from jax.experimental.pallas import tpu as pltpu

F32 = jnp.float32
BF16 = jnp.bfloat16
I32 = jnp.int32
U32 = jnp.uint32

D_MODEL = 2048
DEPTH = 4
D_CONV = D_MODEL // 2
D_ATT = D_MODEL // 2
HEAD_DIM = 128
N_HEADS = D_ATT // HEAD_DIM
N_EXPERTS = 64
TOP_K = 8
N_GROUPS = 8
GROUP_SIZE = N_EXPERTS // N_GROUPS
TOPK_GROUPS = 4
D_EXPERT = 384
ROUTED_SCALE = 2.5
EXPERT_BLOCK = 256
EXPERT_CHUNK = 128
DEEPNORM_ALPHA = (2 * DEPTH) ** 0.25
LN_EPS = 1e-5
RMS_EPS = 1e-6
N_MOD = 6
ATT_SCALE = HEAD_DIM ** -0.5
EXP_UNDERFLOW = -110.0

LANES = 128
SUBLANES = 8
VMEM_LIMIT = 48 * 1024 * 1024

MOD_TN = 1024
INPROJ_TM = 512
INPROJ_TN = 1024
CONV_TM = 512
ATT_BQ = 256
ATT_BK = 256
ATT_HEADS = 2
OUT_TM = 256
ROUTER_TM = 512
COMBINE_TM = 256
COMBINE_ROWS = 64
DISPATCH_TM = 256
DMA_UNROLL = 8
TOKEN_ROWS = (D_MODEL // 2) // LANES


def _params(*sem):
    return pltpu.CompilerParams(dimension_semantics=sem, vmem_limit_bytes=VMEM_LIMIT)


def _mod_kernel(c_ref, w_ref, b_ref, o_ref):
    o_ref[...] = jnp.dot(c_ref[...], w_ref[...], preferred_element_type=F32) + b_ref[...]


def _modulation(c, w_mod, b_mod):
    n = N_MOD * D_MODEL
    c8 = jnp.broadcast_to(c, (SUBLANES, D_MODEL))
    out = pl.pallas_call(
        _mod_kernel,
        out_shape=jax.ShapeDtypeStruct((DEPTH, SUBLANES, n), F32),
        grid=(DEPTH, n // MOD_TN),
        in_specs=[
            pl.BlockSpec((SUBLANES, D_MODEL), lambda l, j: (0, 0)),
            pl.BlockSpec((None, D_MODEL, MOD_TN), lambda l, j: (l, 0, j)),
            pl.BlockSpec((None, 1, MOD_TN), lambda l, j: (l, 0, j)),
        ],
        out_specs=pl.BlockSpec((None, SUBLANES, MOD_TN), lambda l, j: (l, 0, j)),
        compiler_params=_params("arbitrary", "arbitrary"),
        name="modulation",
    )(c8, w_mod, b_mod.reshape(DEPTH, 1, n))
    return out[:, 0:1, :]


def _inproj_kernel(x_ref, sc_ref, sh_ref, w_ref, o_ref, w_bf, *, q_tiles):
    @pl.when(pl.program_id(1) == 0)
    def _():
        w_bf[...] = w_ref[...].astype(BF16)

    u = (x_ref[...] * (1.0 + sc_ref[...]) + sh_ref[...]).astype(BF16)
    acc = jnp.dot(u, w_bf[...], preferred_element_type=F32)
    if q_tiles:
        acc = acc * jnp.where(pl.program_id(0) < q_tiles, ATT_SCALE, 1.0)
    o_ref[...] = acc.astype(o_ref.dtype)


def _inproj(x, sc, sh, w_in, layer, col0, out_dtype, q_tiles):
    t = x.shape[0]
    n = 3 * D_CONV
    tile0 = col0 // INPROJ_TN
    return pl.pallas_call(
        functools.partial(_inproj_kernel, q_tiles=q_tiles),
        out_shape=jax.ShapeDtypeStruct((t, n), out_dtype),
        grid=(n // INPROJ_TN, t // INPROJ_TM),
        in_specs=[
            pl.BlockSpec((INPROJ_TM, D_MODEL), lambda j, i: (i, 0)),
            pl.BlockSpec((1, D_MODEL), lambda j, i: (0, 0)),
            pl.BlockSpec((1, D_MODEL), lambda j, i: (0, 0)),
            pl.BlockSpec((None, D_MODEL, INPROJ_TN), lambda j, i: (layer, 0, tile0 + j)),
        ],
        out_specs=pl.BlockSpec((INPROJ_TM, INPROJ_TN), lambda j, i: (i, j)),
        scratch_shapes=[pltpu.VMEM((D_MODEL, INPROJ_TN), BF16)],
        compiler_params=_params("arbitrary", "arbitrary"),
        name="inproj",
    )(x, sc, sh, w_in)


def _conv_kernel(h_ref, b_ref, c_ref, hh_ref, ch_ref, cw_ref, g_ref, o_ref):
    i = pl.program_id(0)
    v = c_ref[...] * h_ref[...]
    halo = jnp.where(i > 0, ch_ref[...] * hh_ref[...], 0.0)
    row = lax.broadcasted_iota(I32, v.shape, 0)
    prev1 = halo[SUBLANES - 1:SUBLANES, :]
    prev2 = halo[SUBLANES - 2:SUBLANES - 1, :]
    v1 = jnp.where(row == 0, prev1, pltpu.roll(v, 1, axis=0))
    v2 = jnp.where(row == 0, prev2, jnp.where(row == 1, prev1, pltpu.roll(v, 2, axis=0)))
    cw = cw_ref[...]
    conv = cw[0:1, :] * v2 + cw[1:2, :] * v1 + cw[2:3, :] * v
    y = b_ref[...] * conv
    ms = jnp.mean(y * y, axis=-1, keepdims=True)
    o_ref[...] = (y * lax.rsqrt(ms + RMS_EPS) * g_ref[...]).astype(o_ref.dtype)


def _conv(hbc, conv_w, g_conv, layer):
    t = hbc.shape[0]
    halo_blocks = CONV_TM // SUBLANES

    def halo_map(col):
        return lambda i: (jnp.maximum(i * halo_blocks - 1, 0), col)

    return pl.pallas_call(
        _conv_kernel,
        out_shape=jax.ShapeDtypeStruct((t, D_CONV), BF16),
        grid=(t // CONV_TM,),
        in_specs=[
            pl.BlockSpec((CONV_TM, D_CONV), lambda i: (i, 0)),
            pl.BlockSpec((CONV_TM, D_CONV), lambda i: (i, 1)),
            pl.BlockSpec((CONV_TM, D_CONV), lambda i: (i, 2)),
            pl.BlockSpec((SUBLANES, D_CONV), halo_map(0)),
            pl.BlockSpec((SUBLANES, D_CONV), halo_map(2)),
            pl.BlockSpec((None, 3, D_CONV), lambda i: (layer, 0, 0)),
            pl.BlockSpec((None, 1, D_CONV), lambda i: (layer, 0, 0)),
        ],
        out_specs=pl.BlockSpec((CONV_TM, D_CONV), lambda i: (i, 0)),
        compiler_params=_params("arbitrary"),
        name="conv_rms",
    )(hbc, hbc, hbc, hbc, hbc, conv_w, g_conv.reshape(DEPTH, 1, D_CONV))


def _attn_kernel(q_ref, k_ref, v_ref, o_ref, *, bq, bk, heads):
    i = pl.program_id(1)
    row = lax.broadcasted_iota(I32, (bq, bk), 0)
    col = lax.broadcasted_iota(I32, (bq, bk), 1)
    causal = col < row
    r2 = lax.broadcasted_iota(I32, (bk, bk + LANES), 0)
    c2 = lax.broadcasted_iota(I32, (bk, bk + LANES), 1)
    tri = jnp.where((r2 >= c2) | (c2 >= bk), 1.0, 0.0).astype(BF16)

    def block(h, j, later, diagonal):
        cols = slice(h * HEAD_DIM, (h + 1) * HEAD_DIM)
        start = pl.multiple_of(j * bk, bk)
        q = q_ref[:, cols]
        ks = k_ref[pl.ds(start, bk), cols]
        vs = v_ref[pl.ds(start, bk), cols]
        z = lax.dot_general(q, ks, (((1,), (1,)), ((), ())), preferred_element_type=F32)
        log_keep = -(jnp.maximum(z, 0.0) + jnp.log(1.0 + jnp.exp(-jnp.abs(z))))
        if diagonal:
            log_keep = jnp.where(causal, log_keep, 0.0)
        hi = log_keep.astype(BF16)
        lo = (log_keep - hi.astype(F32)).astype(BF16)
        sums = (jnp.dot(hi, tri, preferred_element_type=F32)
                + jnp.dot(lo, tri, preferred_element_type=F32))
        incl = sums[:, :bk] + jnp.concatenate([later] * (bk // LANES), axis=1)
        a = jnp.exp(z + incl)
        if diagonal:
            a = jnp.where(causal, a, 0.0)
        pv = jnp.dot(a.astype(BF16), vs, preferred_element_type=F32)
        return pv, later + sums[:, bk:]

    def any_live(laters):
        top = jnp.max(laters[0])
        for later in laters[1:]:
            top = jnp.maximum(top, jnp.max(later))
        return top > EXP_UNDERFLOW

    first = [block(h, i, jnp.zeros((bq, LANES), F32), True) for h in range(heads)]
    accs = tuple(pv for pv, _ in first)
    laters = tuple(later for _, later in first)

    def cond(c):
        jj, _, _, live = c
        return jnp.logical_and(jj <= i, live)

    def body(c):
        jj, accs, laters, _ = c
        step = [block(h, i - jj, laters[h], False) for h in range(heads)]
        accs = tuple(acc + pv for acc, (pv, _) in zip(accs, step))
        laters = tuple(later for _, later in step)
        return jj + 1, accs, laters, any_live(laters)

    _, accs, _, _ = lax.while_loop(cond, body, (jnp.int32(1), accs, laters, any_live(laters)))
    for h in range(heads):
        o_ref[:, h * HEAD_DIM:(h + 1) * HEAD_DIM] = accs[h]


def _attention(qkv):
    t = qkv.shape[0]
    bq, bk, heads = ATT_BQ, ATT_BK, ATT_HEADS
    assert bq == bk and HEAD_DIM == LANES
    groups = N_HEADS // heads
    width = heads * HEAD_DIM
    return pl.pallas_call(
        functools.partial(_attn_kernel, bq=bq, bk=bk, heads=heads),
        out_shape=jax.ShapeDtypeStruct((t, D_ATT), F32),
        grid=(groups, t // bq),
        in_specs=[
            pl.BlockSpec((bq, width), lambda h, i: (i, h)),
            pl.BlockSpec((t, width), lambda h, i: (0, groups + h)),
            pl.BlockSpec((t, width), lambda h, i: (0, 2 * groups + h)),
        ],
        out_specs=pl.BlockSpec((bq, width), lambda h, i: (i, h)),
        compiler_params=_params("arbitrary", "arbitrary"),
        name="stickbreak_attn",
    )(qkv, qkv, qkv)


def _layer_norm(r, g, b):
    mu = jnp.mean(r, axis=-1, keepdims=True)
    rc = r - mu
    var = jnp.mean(rc * rc, axis=-1, keepdims=True)
    return rc * lax.rsqrt(var + LN_EPS) * g + b


def _outproj_kernel(yc_ref, att_ref, ga_ref, wc_ref, wa_ref, x_ref, g1_ref, lg_ref, lb_ref, o_ref):
    att = att_ref[...]
    ms = jnp.mean(att * att, axis=-1, keepdims=True)
    ya = (att * lax.rsqrt(ms + RMS_EPS) * ga_ref[...]).astype(BF16)
    mix = (jnp.dot(yc_ref[...], wc_ref[...], preferred_element_type=F32)
           + jnp.dot(ya, wa_ref[...], preferred_element_type=F32))
    r = DEEPNORM_ALPHA * x_ref[...] + (1.0 + g1_ref[...]) * mix
    o_ref[...] = _layer_norm(r, lg_ref[...], lb_ref[...])


def _outproj(yc, att, g_att, w_out_bf, x, g1, ln_g, ln_b, layer):
    t = x.shape[0]
    vec = lambda n: pl.BlockSpec((None, 1, n), lambda i: (layer, 0, 0))
    return pl.pallas_call(
        _outproj_kernel,
        out_shape=jax.ShapeDtypeStruct((t, D_MODEL), F32),
        grid=(t // OUT_TM,),
        in_specs=[
            pl.BlockSpec((OUT_TM, D_CONV), lambda i: (i, 0)),
            pl.BlockSpec((OUT_TM, D_ATT), lambda i: (i, 0)),
            vec(D_ATT),
            pl.BlockSpec((None, D_CONV, D_MODEL), lambda i: (layer, 0, 0)),
            pl.BlockSpec((None, D_ATT, D_MODEL), lambda i: (layer, 1, 0)),
            pl.BlockSpec((OUT_TM, D_MODEL), lambda i: (i, 0)),
            pl.BlockSpec((1, D_MODEL), lambda i: (0, 0)),
            vec(D_MODEL),
            vec(D_MODEL),
        ],
        out_specs=pl.BlockSpec((OUT_TM, D_MODEL), lambda i: (i, 0)),
        compiler_params=_params("arbitrary"),
        name="outproj_ln",
    )(yc, att, g_att.reshape(DEPTH, 1, D_ATT), w_out_bf, w_out_bf, x, g1,
      ln_g.reshape(DEPTH, 1, D_MODEL), ln_b.reshape(DEPTH, 1, D_MODEL))


def _split_bf16(a):
    hi = a.astype(BF16)
    lo = (a - hi.astype(F32)).astype(BF16)
    return hi, lo


def _pack_halves(a_bf):
    n = a_bf.shape[1] // 2
    bits = lax.bitcast_convert_type(a_bf.astype(F32), U32)
    return (bits[:, :n] >> 16) | (bits[:, n:] & jnp.uint32(0xFFFF0000))


def _unpack_halves(words):
    lo = lax.bitcast_convert_type(words << 16, F32)
    hi = lax.bitcast_convert_type(words & jnp.uint32(0xFFFF0000), F32)
    return lo, hi


def _store_token_tiles(ref, words, tok0=0):
    m = words.shape[0]
    for g in range(TOKEN_ROWS):
        ref[pl.ds(tok0 * TOKEN_ROWS + g, m, stride=TOKEN_ROWS), :] = words[:, g * LANES:(g + 1) * LANES]


def _load_token_tiles(ref, m, tok0=0):
    return jnp.concatenate(
        [ref[pl.ds(tok0 * TOKEN_ROWS + g, m, stride=TOKEN_ROWS), :] for g in range(TOKEN_ROWS)], axis=1)


def _router_kernel(x_ref, sc_ref, sh_ref, wr_ref, bias_ref,
                   u_ref, eidx_ref, wts_ref, rank_ref, cnt_ref):
    step = pl.program_id(0)
    tm = x_ref.shape[0]
    neg_inf = -jnp.inf

    @pl.when(step == 0)
    def _():
        cnt_ref[...] = jnp.zeros_like(cnt_ref)

    u = x_ref[...] * (1.0 + sc_ref[...]) + sh_ref[...]
    u_hi, u_lo = _split_bf16(u)
    _store_token_tiles(u_ref, _pack_halves(u_hi))
    w_hi, w_lo = _split_bf16(wr_ref[...])
    nt = (((1,), (1,)), ((), ()))
    logits = (lax.dot_general(w_hi, u_hi, nt, preferred_element_type=F32)
              + lax.dot_general(w_hi, u_lo, nt, preferred_element_type=F32)
              + lax.dot_general(w_lo, u_hi, nt, preferred_element_type=F32))
    scores = 1.0 / (1.0 + jnp.exp(-logits))
    sel = scores + bias_ref[...]

    gscore = []
    for g in range(N_GROUPS):
        grp = sel[g * GROUP_SIZE:(g + 1) * GROUP_SIZE, :]
        m1 = jnp.max(grp, axis=0, keepdims=True)
        is_max = grp == m1
        n_max = jnp.sum(is_max.astype(F32), axis=0, keepdims=True)
        rest = jnp.max(jnp.where(is_max, neg_inf, grp), axis=0, keepdims=True)
        gscore.append(m1 + jnp.where(n_max >= 2.0, m1, rest))
    masked = []
    for g in range(N_GROUPS):
        beaten = jnp.zeros((1, tm), F32)
        for o in range(N_GROUPS):
            if o == g:
                continue
            wins = (gscore[o] >= gscore[g]) if o < g else (gscore[o] > gscore[g])
            beaten = beaten + wins.astype(F32)
        keep = beaten < float(TOPK_GROUPS)
        grp = sel[g * GROUP_SIZE:(g + 1) * GROUP_SIZE, :]
        masked.append(jnp.where(keep, grp, neg_inf))
    cand = jnp.concatenate(masked, axis=0)

    eiota = lax.broadcasted_iota(I32, (N_EXPERTS, tm), 0)
    picked = jnp.zeros((N_EXPERTS, tm), F32)
    idxs, ws = [], []
    for _ in range(TOP_K):
        m = jnp.max(cand, axis=0, keepdims=True)
        idx = jnp.min(jnp.where(cand == m, eiota, N_EXPERTS), axis=0, keepdims=True)
        hit = eiota == idx
        ws.append(jnp.sum(jnp.where(hit, scores, 0.0), axis=0, keepdims=True))
        idxs.append(idx)
        picked = jnp.where(hit, 1.0, picked)
        cand = jnp.where(hit, neg_inf, cand)
    wsum = ws[0]
    for k in range(1, TOP_K):
        wsum = wsum + ws[k]
    norm = ROUTED_SCALE / wsum

    r = lax.broadcasted_iota(I32, (tm, tm), 0)
    c = lax.broadcasted_iota(I32, (tm, tm), 1)
    before = jnp.where(r < c, 1.0, 0.0).astype(BF16)
    ones = jnp.ones((tm, LANES), BF16)
    picked_bf = picked.astype(BF16)
    prior = jnp.dot(picked_bf, before, preferred_element_type=F32)
    total = jnp.dot(picked_bf, ones, preferred_element_type=F32)
    carried = cnt_ref[...]
    rank_full = prior + jnp.concatenate([carried] * (tm // LANES), axis=1)
    cnt_ref[...] = carried + total

    for k in range(TOP_K):
        hit = eiota == idxs[k]
        eidx_ref[k:k + 1, :] = idxs[k]
        wts_ref[k:k + 1, :] = ws[k] * norm
        rank_ref[k:k + 1, :] = jnp.sum(jnp.where(hit, rank_full, 0.0), axis=0,
                                       keepdims=True).astype(I32)


def _router(x, sc, sh, w_router_t, router_bias, layer):
    t = x.shape[0]
    tm = ROUTER_TM
    return pl.pallas_call(
        _router_kernel,
        out_shape=(
            jax.ShapeDtypeStruct((t * TOKEN_ROWS, LANES), U32),
            jax.ShapeDtypeStruct((TOP_K, t), I32),
            jax.ShapeDtypeStruct((TOP_K, t), F32),
            jax.ShapeDtypeStruct((TOP_K, t), I32),
            jax.ShapeDtypeStruct((N_EXPERTS, LANES), F32),
        ),
        grid=(t // tm,),
        in_specs=[
            pl.BlockSpec((tm, D_MODEL), lambda i: (i, 0)),
            pl.BlockSpec((1, D_MODEL), lambda i: (0, 0)),
            pl.BlockSpec((1, D_MODEL), lambda i: (0, 0)),
            pl.BlockSpec((None, N_EXPERTS, D_MODEL), lambda i: (layer, 0, 0)),
            pl.BlockSpec((None, N_EXPERTS, 1), lambda i: (layer, 0, 0)),
        ],
        out_specs=(
            pl.BlockSpec((tm * TOKEN_ROWS, LANES), lambda i: (i, 0)),
            pl.BlockSpec((TOP_K, tm), lambda i: (0, i)),
            pl.BlockSpec((TOP_K, tm), lambda i: (0, i)),
            pl.BlockSpec((TOP_K, tm), lambda i: (0, i)),
            pl.BlockSpec((N_EXPERTS, LANES), lambda i: (0, 0)),
        ),
        compiler_params=_params("arbitrary"),
        name="router",
    )(x, sc, sh, w_router_t, router_bias.reshape(DEPTH, N_EXPERTS, 1))


def _dest_kernel(pad_start_ref, eidx_ref, rank_ref, dest_ref):
    eidx = eidx_ref[...]
    dest = rank_ref[...]
    for e in range(N_EXPERTS):
        dest = dest + jnp.where(eidx == e, pad_start_ref[e], 0)
    dest_ref[...] = dest


def _dest_rows(pad_start, eidx, rank):
    k, t = eidx.shape
    grid_spec = pltpu.PrefetchScalarGridSpec(
        num_scalar_prefetch=1,
        grid=(1,),
        in_specs=[pl.BlockSpec((k, t), lambda i, p: (0, 0)), pl.BlockSpec((k, t), lambda i, p: (0, 0))],
        out_specs=pl.BlockSpec((k, t), lambda i, p: (0, 0)),
    )
    return pl.pallas_call(
        _dest_kernel,
        out_shape=jax.ShapeDtypeStruct((k, t), I32),
        grid_spec=grid_spec,
        compiler_params=_params("arbitrary"),
        name="dest_rows",
    )(pad_start, eidx, rank)


def _token_copy(src, src_tok, dst, dst_tok, sem):
    s = pl.ds(pl.multiple_of(src_tok * TOKEN_ROWS, TOKEN_ROWS), TOKEN_ROWS)
    d = pl.ds(pl.multiple_of(dst_tok * TOKEN_ROWS, TOKEN_ROWS), TOKEN_ROWS)
    return pltpu.make_async_copy(src.at[s, :], dst.at[d, :], sem)


def _block_copy(src, dst_hbm, blk, sem):
    rows = src.shape[0]
    return pltpu.make_async_copy(src, dst_hbm.at[pl.ds(pl.multiple_of(blk * rows, rows), rows), :], sem)


def _dispatch_kernel(dest_ref, zflag_ref, u_ref, xg_hbm, zbuf, sem, zsem, *, n_tok, n_blocks):
    i = pl.program_id(0)
    tm = u_ref.shape[0] // TOKEN_ROWS

    @pl.when(i == 0)
    def _():
        zbuf[...] = jnp.zeros_like(zbuf)

        def zero_start(b, _):
            @pl.when(zflag_ref[b] != 0)
            def _():
                _block_copy(zbuf, xg_hbm, b, zsem).start()
            return 0

        def zero_wait(b, _):
            @pl.when(zflag_ref[b] != 0)
            def _():
                _block_copy(zbuf, xg_hbm, b, zsem).wait()
            return 0

        lax.fori_loop(0, n_blocks, zero_start, 0)
        lax.fori_loop(0, n_blocks, zero_wait, 0)

    for k in range(TOP_K):
        base = k * n_tok + i * tm

        def issue(it, _, base=base):
            r0 = it * DMA_UNROLL
            for j in range(DMA_UNROLL):
                _token_copy(u_ref, r0 + j, xg_hbm, dest_ref[base + r0 + j], sem).start(priority=j % 2)
            return 0

        lax.fori_loop(0, tm // DMA_UNROLL, issue, 0)

    def drain(it, _):
        for _j in range(DMA_UNROLL):
            _token_copy(u_ref, 0, xg_hbm, 0, sem).wait()
        return 0

    lax.fori_loop(0, TOP_K * tm // DMA_UNROLL, drain, 0)


def _dispatch(dest_flat, zflag, u_packed, n_blocks):
    t = u_packed.shape[0] // TOKEN_ROWS
    tm = DISPATCH_TM
    grid_spec = pltpu.PrefetchScalarGridSpec(
        num_scalar_prefetch=2,
        grid=(t // tm,),
        in_specs=[pl.BlockSpec((tm * TOKEN_ROWS, LANES), lambda i, d, z: (i, 0))],
        out_specs=pl.BlockSpec(memory_space=pl.ANY),
        scratch_shapes=[
            pltpu.VMEM((EXPERT_BLOCK * TOKEN_ROWS, LANES), U32),
            pltpu.SemaphoreType.DMA(()),
            pltpu.SemaphoreType.DMA(()),
        ],
    )
    return pl.pallas_call(
        functools.partial(_dispatch_kernel, n_tok=t, n_blocks=n_blocks),
        out_shape=jax.ShapeDtypeStruct((n_blocks * EXPERT_BLOCK * TOKEN_ROWS, LANES), U32),
        grid_spec=grid_spec,
        compiler_params=_params("arbitrary"),
        name="dispatch",
    )(dest_flat, zflag, u_packed)


def _silu_gate(g, up):
    return (g * (1.0 / (1.0 + jnp.exp(-g))) * up).astype(BF16)


def _expert_kernel(be_ref, nused_ref, nxt_ref, slot_ref, x_ref, wg_hbm, wu_hbm, wd_hbm, o_ref,
                   wg_buf, wu_buf, wd_buf, wgu_bf, wd_bf, sems, *, layer):
    b = pl.program_id(0)

    def weight_copies(expert, slot):
        return (pltpu.make_async_copy(wg_hbm.at[layer, expert], wg_buf.at[slot], sems.at[slot]),
                pltpu.make_async_copy(wu_hbm.at[layer, expert], wu_buf.at[slot], sems.at[slot]),
                pltpu.make_async_copy(wd_hbm.at[layer, expert], wd_buf.at[slot], sems.at[slot]))

    @pl.when(b < nused_ref[0])
    def _():
        expert = be_ref[b]
        slot = slot_ref[b]

        @pl.when(b == 0)
        def _():
            for cp in weight_copies(expert, slot):
                cp.start()

        @pl.when(jnp.logical_or(b == 0, expert != be_ref[jnp.maximum(b - 1, 0)]))
        def _():
            nxt = nxt_ref[b]

            @pl.when(nxt >= 0)
            def _():
                for cp in weight_copies(nxt, 1 - slot):
                    cp.start()

            for cp in weight_copies(expert, slot):
                cp.wait()
            wgu_bf[:, :D_EXPERT] = wg_buf[slot].astype(BF16)
            wgu_bf[:, D_EXPERT:] = wu_buf[slot].astype(BF16)
            wd_bf[...] = wd_buf[slot].astype(BF16)

        for tok0 in range(0, EXPERT_BLOCK, EXPERT_CHUNK):
            lo, hi = _unpack_halves(_load_token_tiles(x_ref, EXPERT_CHUNK, tok0))
            x = jnp.concatenate([lo.astype(BF16), hi.astype(BF16)], axis=1)
            gu = jnp.dot(x, wgu_bf[...], preferred_element_type=F32)
            h = _silu_gate(gu[:, :D_EXPERT], gu[:, D_EXPERT:])
            out = jnp.dot(h, wd_bf[...], preferred_element_type=F32)
            _store_token_tiles(o_ref, _pack_halves(out.astype(BF16)), tok0)

    @pl.when(b >= nused_ref[0])
    def _():
        o_ref[...] = jnp.zeros_like(o_ref)


def _experts(blk_expert, n_used, blk_next, blk_slot, xg, w_gate, w_up, w_down, layer, n_blocks):
    blk_rows = EXPERT_BLOCK * TOKEN_ROWS
    hbm = lambda: pl.BlockSpec(memory_space=pl.ANY)
    grid_spec = pltpu.PrefetchScalarGridSpec(
        num_scalar_prefetch=4,
        grid=(n_blocks,),
        in_specs=[
            pl.BlockSpec((blk_rows, LANES), lambda b, be, nu, nx, sl: (jnp.minimum(b, nu[0] - 1), 0)),
            hbm(), hbm(), hbm(),
        ],
        out_specs=pl.BlockSpec((blk_rows, LANES), lambda b, be, nu, nx, sl: (b, 0)),
        scratch_shapes=[
            pltpu.VMEM((2, D_MODEL, D_EXPERT), F32),
            pltpu.VMEM((2, D_MODEL, D_EXPERT), F32),
            pltpu.VMEM((2, D_EXPERT, D_MODEL), F32),
            pltpu.VMEM((D_MODEL, 2 * D_EXPERT), BF16),
            pltpu.VMEM((D_EXPERT, D_MODEL), BF16),
            pltpu.SemaphoreType.DMA((2,)),
        ],
    )
    return pl.pallas_call(
        functools.partial(_expert_kernel, layer=layer),
        out_shape=jax.ShapeDtypeStruct((n_blocks * blk_rows, LANES), U32),
        grid_spec=grid_spec,
        compiler_params=_params("arbitrary"),
        name="experts",
    )(blk_expert, n_used, blk_next, blk_slot, xg, w_gate, w_up, w_down)


def _combine_kernel(dest_ref, eo_hbm, wts_ref, x_ref, sc_ref, sh_ref, g2_ref,
                    sg_ref, su_ref, sd_ref, lg_ref, lb_ref, o_ref, gbuf, sems, *, n_tok):
    i = pl.program_id(0)
    tm = x_ref.shape[0]
    half = D_MODEL // 2
    slot = i % 2

    def gather_tile(tile, into):
        for k in range(TOP_K):
            base = k * n_tok + tile * tm

            def issue(it, _, k=k, base=base):
                r0 = it * DMA_UNROLL
                for j in range(DMA_UNROLL):
                    _token_copy(eo_hbm, dest_ref[base + r0 + j], gbuf.at[into, k], r0 + j,
                                sems.at[into]).start(priority=j % 2)
                return 0

            lax.fori_loop(0, tm // DMA_UNROLL, issue, 0)

    @pl.when(i == 0)
    def _():
        gather_tile(0, 0)

    @pl.when(i + 1 < pl.num_programs(0))
    def _():
        gather_tile(i + 1, 1 - slot)

    def drain(it, _):
        for _j in range(DMA_UNROLL):
            _token_copy(eo_hbm, 0, gbuf.at[slot, 0], 0, sems.at[slot]).wait()
        return 0

    lax.fori_loop(0, TOP_K * tm // DMA_UNROLL, drain, 0)

    def routed_rows(rc, _):
        r0 = pl.multiple_of(rc * COMBINE_ROWS, COMBINE_ROWS)
        rows = pl.ds(r0, COMBINE_ROWS)
        wts = wts_ref[rows, :]
        for g in range(TOKEN_ROWS):
            acc_lo = acc_hi = None
            for k in range(TOP_K):
                words = gbuf[slot, k, pl.ds(r0 * TOKEN_ROWS + g, COMBINE_ROWS, stride=TOKEN_ROWS), :]
                lo, hi = _unpack_halves(words)
                w = wts[:, k:k + 1]
                acc_lo = w * lo if acc_lo is None else acc_lo + w * lo
                acc_hi = w * hi if acc_hi is None else acc_hi + w * hi
            o_ref[rows, g * LANES:(g + 1) * LANES] = acc_lo
            o_ref[rows, half + g * LANES:half + (g + 1) * LANES] = acc_hi
        return 0

    lax.fori_loop(0, tm // COMBINE_ROWS, routed_rows, 0)
    x = x_ref[...]
    u = (x * (1.0 + sc_ref[...]) + sh_ref[...]).astype(BF16)
    h = _silu_gate(jnp.dot(u, sg_ref[...], preferred_element_type=F32),
                   jnp.dot(u, su_ref[...], preferred_element_type=F32))
    shared = jnp.dot(h, sd_ref[...], preferred_element_type=F32)
    ffn = shared + o_ref[...]
    r = DEEPNORM_ALPHA * x + (1.0 + g2_ref[...]) * ffn
    o_ref[...] = _layer_norm(r, lg_ref[...], lb_ref[...])


def _combine(dest_flat, eo, wts_tk, x, sc, sh, g2, sg_bf, su_bf, sd_bf, ln_g, ln_b, layer):
    t = x.shape[0]
    tm = COMBINE_TM
    vec = lambda n: pl.BlockSpec((None, 1, n), lambda i, d: (layer, 0, 0))
    row = lambda: pl.BlockSpec((1, D_MODEL), lambda i, d: (0, 0))
    grid_spec = pltpu.PrefetchScalarGridSpec(
        num_scalar_prefetch=1,
        grid=(t // tm,),
        in_specs=[
            pl.BlockSpec(memory_space=pl.ANY),
            pl.BlockSpec((tm, TOP_K), lambda i, d: (i, 0)),
            pl.BlockSpec((tm, D_MODEL), lambda i, d: (i, 0)),
            row(), row(), row(),
            pl.BlockSpec((None, D_MODEL, D_EXPERT), lambda i, d: (layer, 0, 0)),
            pl.BlockSpec((None, D_MODEL, D_EXPERT), lambda i, d: (layer, 0, 0)),
            pl.BlockSpec((None, D_EXPERT, D_MODEL), lambda i, d: (layer, 0, 0)),
            vec(D_MODEL), vec(D_MODEL),
        ],
        out_specs=pl.BlockSpec((tm, D_MODEL), lambda i, d: (i, 0)),
        scratch_shapes=[
            pltpu.VMEM((2, TOP_K, tm * TOKEN_ROWS, LANES), U32),
            pltpu.SemaphoreType.DMA((2,)),
        ],
    )
    return pl.pallas_call(
        functools.partial(_combine_kernel, n_tok=t),
        out_shape=jax.ShapeDtypeStruct((t, D_MODEL), F32),
        grid_spec=grid_spec,
        compiler_params=_params("arbitrary"),
        name="combine_ln",
    )(dest_flat, eo, wts_tk, x, sc, sh, g2, sg_bf, su_bf, sd_bf,
      ln_g.reshape(DEPTH, 1, D_MODEL), ln_b.reshape(DEPTH, 1, D_MODEL))


def _block_plan(counts, n_blocks):
    blocks_per_e = (counts + EXPERT_BLOCK - 1) // EXPERT_BLOCK
    blk_end = jnp.cumsum(blocks_per_e)
    pad_start = ((blk_end - blocks_per_e) * EXPERT_BLOCK).astype(I32)
    n_used = blk_end[-1:].astype(I32)
    blk = jnp.arange(n_blocks, dtype=I32)[:, None]
    blk_expert = jnp.minimum(jnp.sum((blk_end[None, :] <= blk).astype(I32), axis=1), N_EXPERTS - 1)
    partial_blk = jnp.any((blk == blk_end[None, :] - 1) & (blocks_per_e[None, :] > 0), axis=1)
    zflag = (partial_blk | (blk[:, 0] >= n_used[0])).astype(I32)
    eid = jnp.arange(N_EXPERTS, dtype=I32)
    nonempty = blocks_per_e > 0
    parity = (jnp.cumsum(nonempty.astype(I32)) - 1) & 1
    later = nonempty[None, :] & (eid[None, :] > eid[:, None])
    next_e = jnp.min(jnp.where(later, eid[None, :], N_EXPERTS), axis=1)
    next_e = jnp.where(next_e == N_EXPERTS, -1, next_e)
    onehot = (blk_expert[:, None] == eid[None, :]).astype(I32)
    blk_slot = jnp.sum(onehot * parity[None, :], axis=1).astype(I32)
    blk_next = jnp.sum(onehot * next_e[None, :], axis=1).astype(I32)
    return pad_start, blk_expert.astype(I32), n_used, zflag, blk_next, blk_slot


def kernel(x, c, w_mod, b_mod, w_in, conv_w, g_conv, g_att, w_out, ln1_g, ln1_b, w_router,
           router_bias, w_gate, w_up, w_down, ws_gate, ws_up, ws_down, ln2_g, ln2_b):
    bsz, seq, d = x.shape
    assert bsz == 1 and d == D_MODEL
    t = bsz * seq
    n_blocks = -(-(t * TOP_K) // EXPERT_BLOCK) + N_EXPERTS
    xt = x.reshape(t, d)

    mod = _modulation(c, w_mod, b_mod)
    w_out_bf = w_out.astype(BF16)
    sg_bf, su_bf, sd_bf = ws_gate.astype(BF16), ws_up.astype(BF16), ws_down.astype(BF16)
    w_router_t = jnp.swapaxes(w_router, 1, 2)

    for l in range(DEPTH):
        sh1, sc1, g1, sh2, sc2, g2 = [mod[l, :, n * d:(n + 1) * d] for n in range(N_MOD)]
        hbc = _inproj(xt, sc1, sh1, w_in, l, 0, F32, 0)
        qkv = _inproj(xt, sc1, sh1, w_in, l, 3 * D_CONV, BF16, D_ATT // INPROJ_TN)
        yc = _conv(hbc, conv_w, g_conv, l)
        att = _attention(qkv)
        xt = _outproj(yc, att, g_att, w_out_bf, xt, g1, ln1_g, ln1_b, l)
        u_packed, eidx, wts, rank, cnt = _router(xt, sc2, sh2, w_router_t, router_bias, l)
        pad_start, blk_expert, n_used, zflag, blk_next, blk_slot = _block_plan(
            cnt[:, 0].astype(I32), n_blocks)
        dest = _dest_rows(pad_start, eidx, rank).reshape(-1)
        xg = _dispatch(dest, zflag, u_packed, n_blocks)
        eo = _experts(blk_expert, n_used, blk_next, blk_slot, xg, w_gate, w_up, w_down, l, n_blocks)
        xt = _combine(dest, eo, wts.T, xt, sc2, sh2, g2, sg_bf, su_bf, sd_bf, ln2_g, ln2_b, l)
    return xt.reshape(bsz, seq, d)
```

```python
import functools

import jax
import jax.numpy as jnp
from jax import lax
from jax.experimental import pallas as pl
from jax.experimental.pallas import tpu as pltpu

F32 = jnp.float32
BF16 = jnp.bfloat16
I32 = jnp.int32
U32 = jnp.uint32

D_MODEL = 2048
DEPTH = 4
D_CONV = D_MODEL // 2
D_ATT = D_MODEL // 2
HEAD_DIM = 128
N_HEADS = D_ATT // HEAD_DIM
N_EXPERTS = 64
TOP_K = 8
N_GROUPS = 8
GROUP_SIZE = N_EXPERTS // N_GROUPS
TOPK_GROUPS = 4
D_EXPERT = 384
ROUTED_SCALE = 2.5
EXPERT_BLOCK = 256
EXPERT_CHUNK = 128
DEEPNORM_ALPHA = (2 * DEPTH) ** 0.25
LN_EPS = 1e-5
RMS_EPS = 1e-6
N_MOD = 6
ATT_SCALE = HEAD_DIM ** -0.5
EXP_UNDERFLOW = -110.0

LANES = 128
SUBLANES = 8
VMEM_LIMIT = 48 * 1024 * 1024

MOD_TN = 1024
INPROJ_TM = 512
INPROJ_TN = 1024
CONV_TM = 512
ATT_BQ = 256
ATT_BK = 256
ATT_HEADS = 4
OUT_TM = 256
ROUTER_TM = 512
COMBINE_TM = 256
COMBINE_ROWS = 8
DISPATCH_TM = 256
DMA_UNROLL = 8
TOKEN_ROWS = (D_MODEL // 2) // LANES


def _params(*sem):
    return pltpu.CompilerParams(dimension_semantics=sem, vmem_limit_bytes=VMEM_LIMIT)


def _mod_kernel(c_ref, w_ref, b_ref, o_ref):
    o_ref[...] = jnp.dot(c_ref[...], w_ref[...], preferred_element_type=F32) + b_ref[...]


def _modulation(c, w_mod, b_mod):
    n = N_MOD * D_MODEL
    c8 = jnp.broadcast_to(c, (SUBLANES, D_MODEL))
    out = pl.pallas_call(
        _mod_kernel,
        out_shape=jax.ShapeDtypeStruct((DEPTH, SUBLANES, n), F32),
        grid=(DEPTH, n // MOD_TN),
        in_specs=[
            pl.BlockSpec((SUBLANES, D_MODEL), lambda l, j: (0, 0)),
            pl.BlockSpec((None, D_MODEL, MOD_TN), lambda l, j: (l, 0, j)),
            pl.BlockSpec((None, 1, MOD_TN), lambda l, j: (l, 0, j)),
        ],
        out_specs=pl.BlockSpec((None, SUBLANES, MOD_TN), lambda l, j: (l, 0, j)),
        compiler_params=_params("arbitrary", "arbitrary"),
        name="modulation",
    )(c8, w_mod, b_mod.reshape(DEPTH, 1, n))
    return out[:, 0:1, :]


def _inproj_kernel(x_ref, sc_ref, sh_ref, w_ref, o_ref, w_bf, *, q_tiles):
    @pl.when(pl.program_id(1) == 0)
    def _():
        w_bf[...] = w_ref[...].astype(BF16)

    u = (x_ref[...] * (1.0 + sc_ref[...]) + sh_ref[...]).astype(BF16)
    acc = jnp.dot(u, w_bf[...], preferred_element_type=F32)
    if q_tiles:
        acc = acc * jnp.where(pl.program_id(0) < q_tiles, ATT_SCALE, 1.0)
    o_ref[...] = acc.astype(o_ref.dtype)


def _inproj(x, sc, sh, w_in, layer, col0, out_dtype, q_tiles):
    t = x.shape[0]
    n = 3 * D_CONV
    tile0 = col0 // INPROJ_TN
    return pl.pallas_call(
        functools.partial(_inproj_kernel, q_tiles=q_tiles),
        out_shape=jax.ShapeDtypeStruct((t, n), out_dtype),
        grid=(n // INPROJ_TN, t // INPROJ_TM),
        in_specs=[
            pl.BlockSpec((INPROJ_TM, D_MODEL), lambda j, i: (i, 0)),
            pl.BlockSpec((1, D_MODEL), lambda j, i: (0, 0)),
            pl.BlockSpec((1, D_MODEL), lambda j, i: (0, 0)),
            pl.BlockSpec((None, D_MODEL, INPROJ_TN), lambda j, i: (layer, 0, tile0 + j)),
        ],
        out_specs=pl.BlockSpec((INPROJ_TM, INPROJ_TN), lambda j, i: (i, j)),
        scratch_shapes=[pltpu.VMEM((D_MODEL, INPROJ_TN), BF16)],
        compiler_params=_params("arbitrary", "arbitrary"),
        name="inproj",
    )(x, sc, sh, w_in)


def _conv_kernel(h_ref, b_ref, c_ref, hh_ref, ch_ref, cw_ref, g_ref, o_ref):
    i = pl.program_id(0)
    v = c_ref[...] * h_ref[...]
    halo = jnp.where(i > 0, ch_ref[...] * hh_ref[...], 0.0)
    row = lax.broadcasted_iota(I32, v.shape, 0)
    prev1 = halo[SUBLANES - 1:SUBLANES, :]
    prev2 = halo[SUBLANES - 2:SUBLANES - 1, :]
    v1 = jnp.where(row == 0, prev1, pltpu.roll(v, 1, axis=0))
    v2 = jnp.where(row == 0, prev2, jnp.where(row == 1, prev1, pltpu.roll(v, 2, axis=0)))
    cw = cw_ref[...]
    conv = cw[0:1, :] * v2 + cw[1:2, :] * v1 + cw[2:3, :] * v
    y = b_ref[...] * conv
    ms = jnp.mean(y * y, axis=-1, keepdims=True)
    o_ref[...] = (y * lax.rsqrt(ms + RMS_EPS) * g_ref[...]).astype(o_ref.dtype)


def _conv(hbc, conv_w, g_conv, layer):
    t = hbc.shape[0]
    halo_blocks = CONV_TM // SUBLANES

    def halo_map(col):
        return lambda i: (jnp.maximum(i * halo_blocks - 1, 0), col)

    return pl.pallas_call(
        _conv_kernel,
        out_shape=jax.ShapeDtypeStruct((t, D_CONV), BF16),
        grid=(t // CONV_TM,),
        in_specs=[
            pl.BlockSpec((CONV_TM, D_CONV), lambda i: (i, 0)),
            pl.BlockSpec((CONV_TM, D_CONV), lambda i: (i, 1)),
            pl.BlockSpec((CONV_TM, D_CONV), lambda i: (i, 2)),
            pl.BlockSpec((SUBLANES, D_CONV), halo_map(0)),
            pl.BlockSpec((SUBLANES, D_CONV), halo_map(2)),
            pl.BlockSpec((None, 3, D_CONV), lambda i: (layer, 0, 0)),
            pl.BlockSpec((None, 1, D_CONV), lambda i: (layer, 0, 0)),
        ],
        out_specs=pl.BlockSpec((CONV_TM, D_CONV), lambda i: (i, 0)),
        compiler_params=_params("arbitrary"),
        name="conv_rms",
    )(hbc, hbc, hbc, hbc, hbc, conv_w, g_conv.reshape(DEPTH, 1, D_CONV))


def _attn_kernel(q_ref, k_ref, v_ref, o_ref, *, bq, bk, heads):
    i = pl.program_id(1)
    row = lax.broadcasted_iota(I32, (bq, bk), 0)
    col = lax.broadcasted_iota(I32, (bq, bk), 1)
    causal = col < row
    r2 = lax.broadcasted_iota(I32, (bk, bk + LANES), 0)
    c2 = lax.broadcasted_iota(I32, (bk, bk + LANES), 1)
    tri = jnp.where((r2 >= c2) | (c2 >= bk), 1.0, 0.0).astype(BF16)

    def block(h, j, later, diagonal):
        cols = slice(h * HEAD_DIM, (h + 1) * HEAD_DIM)
        start = pl.multiple_of(j * bk, bk)
        q = q_ref[:, cols]
        ks = k_ref[pl.ds(start, bk), cols]
        vs = v_ref[pl.ds(start, bk), cols]
        z = lax.dot_general(q, ks, (((1,), (1,)), ((), ())), preferred_element_type=F32)
        log_keep = -(jnp.maximum(z, 0.0) + jnp.log(1.0 + jnp.exp(-jnp.abs(z))))
        if diagonal:
            log_keep = jnp.where(causal, log_keep, 0.0)
        hi = log_keep.astype(BF16)
        lo = (log_keep - hi.astype(F32)).astype(BF16)
        sums = (jnp.dot(hi, tri, preferred_element_type=F32)
                + jnp.dot(lo, tri, preferred_element_type=F32))
        incl = sums[:, :bk] + jnp.concatenate([later] * (bk // LANES), axis=1)
        a = jnp.exp(z + incl)
        if diagonal:
            a = jnp.where(causal, a, 0.0)
        pv = jnp.dot(a.astype(BF16), vs, preferred_element_type=F32)
        return pv, later + sums[:, bk:]

    def any_live(laters):
        top = jnp.max(laters[0])
        for later in laters[1:]:
            top = jnp.maximum(top, jnp.max(later))
        return top > EXP_UNDERFLOW

    first = [block(h, i, jnp.zeros((bq, LANES), F32), True) for h in range(heads)]
    accs = tuple(pv for pv, _ in first)
    laters = tuple(later for _, later in first)

    def cond(c):
        jj, _, _, live = c
        return jnp.logical_and(jj <= i, live)

    def body(c):
        jj, accs, laters, _ = c
        step = [block(h, i - jj, laters[h], False) for h in range(heads)]
        accs = tuple(acc + pv for acc, (pv, _) in zip(accs, step))
        laters = tuple(later for _, later in step)
        return jj + 1, accs, laters, any_live(laters)

    _, accs, _, _ = lax.while_loop(cond, body, (jnp.int32(1), accs, laters, any_live(laters)))
    for h in range(heads):
        o_ref[:, h * HEAD_DIM:(h + 1) * HEAD_DIM] = accs[h]


def _attention(qkv):
    t = qkv.shape[0]
    bq, bk, heads = ATT_BQ, ATT_BK, ATT_HEADS
    assert bq == bk and HEAD_DIM == LANES
    groups = N_HEADS // heads
    width = heads * HEAD_DIM
    return pl.pallas_call(
        functools.partial(_attn_kernel, bq=bq, bk=bk, heads=heads),
        out_shape=jax.ShapeDtypeStruct((t, D_ATT), F32),
        grid=(groups, t // bq),
        in_specs=[
            pl.BlockSpec((bq, width), lambda h, i: (i, h)),
            pl.BlockSpec((t, width), lambda h, i: (0, groups + h)),
            pl.BlockSpec((t, width), lambda h, i: (0, 2 * groups + h)),
        ],
        out_specs=pl.BlockSpec((bq, width), lambda h, i: (i, h)),
        compiler_params=_params("arbitrary", "arbitrary"),
        name="stickbreak_attn",
    )(qkv, qkv, qkv)


def _layer_norm(r, g, b):
    mu = jnp.mean(r, axis=-1, keepdims=True)
    rc = r - mu
    var = jnp.mean(rc * rc, axis=-1, keepdims=True)
    return rc * lax.rsqrt(var + LN_EPS) * g + b


def _outproj_kernel(yc_ref, att_ref, ga_ref, wc_ref, wa_ref, x_ref, g1_ref, lg_ref, lb_ref, o_ref):
    att = att_ref[...]
    ms = jnp.mean(att * att, axis=-1, keepdims=True)
    ya = (att * lax.rsqrt(ms + RMS_EPS) * ga_ref[...]).astype(BF16)
    mix = (jnp.dot(yc_ref[...], wc_ref[...], preferred_element_type=F32)
           + jnp.dot(ya, wa_ref[...], preferred_element_type=F32))
    r = DEEPNORM_ALPHA * x_ref[...] + (1.0 + g1_ref[...]) * mix
    o_ref[...] = _layer_norm(r, lg_ref[...], lb_ref[...])


def _outproj(yc, att, g_att, w_out_bf, x, g1, ln_g, ln_b, layer):
    t = x.shape[0]
    vec = lambda n: pl.BlockSpec((None, 1, n), lambda i: (layer, 0, 0))
    return pl.pallas_call(
        _outproj_kernel,
        out_shape=jax.ShapeDtypeStruct((t, D_MODEL), F32),
        grid=(t // OUT_TM,),
        in_specs=[
            pl.BlockSpec((OUT_TM, D_CONV), lambda i: (i, 0)),
            pl.BlockSpec((OUT_TM, D_ATT), lambda i: (i, 0)),
            vec(D_ATT),
            pl.BlockSpec((None, D_CONV, D_MODEL), lambda i: (layer, 0, 0)),
            pl.BlockSpec((None, D_ATT, D_MODEL), lambda i: (layer, 1, 0)),
            pl.BlockSpec((OUT_TM, D_MODEL), lambda i: (i, 0)),
            pl.BlockSpec((1, D_MODEL), lambda i: (0, 0)),
            vec(D_MODEL),
            vec(D_MODEL),
        ],
        out_specs=pl.BlockSpec((OUT_TM, D_MODEL), lambda i: (i, 0)),
        compiler_params=_params("arbitrary"),
        name="outproj_ln",
    )(yc, att, g_att.reshape(DEPTH, 1, D_ATT), w_out_bf, w_out_bf, x, g1,
      ln_g.reshape(DEPTH, 1, D_MODEL), ln_b.reshape(DEPTH, 1, D_MODEL))


def _split_bf16(a):
    hi = a.astype(BF16)
    lo = (a - hi.astype(F32)).astype(BF16)
    return hi, lo


def _pack_halves(a_bf):
    n = a_bf.shape[1] // 2
    bits = lax.bitcast_convert_type(a_bf.astype(F32), U32)
    return (bits[:, :n] >> 16) | (bits[:, n:] & jnp.uint32(0xFFFF0000))


def _unpack_halves(words):
    lo = lax.bitcast_convert_type(words << 16, F32)
    hi = lax.bitcast_convert_type(words & jnp.uint32(0xFFFF0000), F32)
    return lo, hi


def _store_token_tiles(ref, words, tok0=0):
    m = words.shape[0]
    for g in range(TOKEN_ROWS):
        ref[pl.ds(tok0 * TOKEN_ROWS + g, m, stride=TOKEN_ROWS), :] = words[:, g * LANES:(g + 1) * LANES]


def _load_token_tiles(ref, m, tok0=0):
    return jnp.concatenate(
        [ref[pl.ds(tok0 * TOKEN_ROWS + g, m, stride=TOKEN_ROWS), :] for g in range(TOKEN_ROWS)], axis=1)


def _router_kernel(x_ref, sc_ref, sh_ref, wr_ref, bias_ref,
                   u_ref, eidx_ref, wts_ref, rank_ref, cnt_ref):
    step = pl.program_id(0)
    tm = x_ref.shape[0]
    neg_inf = -jnp.inf

    @pl.when(step == 0)
    def _():
        cnt_ref[...] = jnp.zeros_like(cnt_ref)

    u = x_ref[...] * (1.0 + sc_ref[...]) + sh_ref[...]
    u_hi, u_lo = _split_bf16(u)
    _store_token_tiles(u_ref, _pack_halves(u_hi))
    w_hi, w_lo = _split_bf16(wr_ref[...])
    nt = (((1,), (1,)), ((), ()))
    logits = (lax.dot_general(w_hi, u_hi, nt, preferred_element_type=F32)
              + lax.dot_general(w_hi, u_lo, nt, preferred_element_type=F32)
              + lax.dot_general(w_lo, u_hi, nt, preferred_element_type=F32))
    scores = 1.0 / (1.0 + jnp.exp(-logits))
    sel = scores + bias_ref[...]

    gscore = []
    for g in range(N_GROUPS):
        grp = sel[g * GROUP_SIZE:(g + 1) * GROUP_SIZE, :]
        m1 = jnp.max(grp, axis=0, keepdims=True)
        is_max = grp == m1
        n_max = jnp.sum(is_max.astype(F32), axis=0, keepdims=True)
        rest = jnp.max(jnp.where(is_max, neg_inf, grp), axis=0, keepdims=True)
        gscore.append(m1 + jnp.where(n_max >= 2.0, m1, rest))
    masked = []
    for g in range(N_GROUPS):
        beaten = jnp.zeros((1, tm), F32)
        for o in range(N_GROUPS):
            if o == g:
                continue
            wins = (gscore[o] >= gscore[g]) if o < g else (gscore[o] > gscore[g])
            beaten = beaten + wins.astype(F32)
        keep = beaten < float(TOPK_GROUPS)
        grp = sel[g * GROUP_SIZE:(g + 1) * GROUP_SIZE, :]
        masked.append(jnp.where(keep, grp, neg_inf))
    cand = jnp.concatenate(masked, axis=0)

    eiota = lax.broadcasted_iota(I32, (N_EXPERTS, tm), 0)
    picked = jnp.zeros((N_EXPERTS, tm), F32)
    idxs, ws = [], []
    for _ in range(TOP_K):
        m = jnp.max(cand, axis=0, keepdims=True)
        idx = jnp.min(jnp.where(cand == m, eiota, N_EXPERTS), axis=0, keepdims=True)
        hit = eiota == idx
        ws.append(jnp.sum(jnp.where(hit, scores, 0.0), axis=0, keepdims=True))
        idxs.append(idx)
        picked = jnp.where(hit, 1.0, picked)
        cand = jnp.where(hit, neg_inf, cand)
    wsum = ws[0]
    for k in range(1, TOP_K):
        wsum = wsum + ws[k]
    norm = ROUTED_SCALE / wsum

    r = lax.broadcasted_iota(I32, (tm, tm), 0)
    c = lax.broadcasted_iota(I32, (tm, tm), 1)
    before = jnp.where(r < c, 1.0, 0.0).astype(BF16)
    ones = jnp.ones((tm, LANES), BF16)
    picked_bf = picked.astype(BF16)
    prior = jnp.dot(picked_bf, before, preferred_element_type=F32)
    total = jnp.dot(picked_bf, ones, preferred_element_type=F32)
    carried = cnt_ref[...]
    rank_full = prior + jnp.concatenate([carried] * (tm // LANES), axis=1)
    cnt_ref[...] = carried + total

    for k in range(TOP_K):
        hit = eiota == idxs[k]
        eidx_ref[k:k + 1, :] = idxs[k]
        wts_ref[k:k + 1, :] = ws[k] * norm
        rank_ref[k:k + 1, :] = jnp.sum(jnp.where(hit, rank_full, 0.0), axis=0,
                                       keepdims=True).astype(I32)


def _router(x, sc, sh, w_router_t, router_bias, layer):
    t = x.shape[0]
    tm = ROUTER_TM
    return pl.pallas_call(
        _router_kernel,
        out_shape=(
            jax.ShapeDtypeStruct((t * TOKEN_ROWS, LANES), U32),
            jax.ShapeDtypeStruct((TOP_K, t), I32),
            jax.ShapeDtypeStruct((TOP_K, t), F32),
            jax.ShapeDtypeStruct((TOP_K, t), I32),
            jax.ShapeDtypeStruct((N_EXPERTS, LANES), F32),
        ),
        grid=(t // tm,),
        in_specs=[
            pl.BlockSpec((tm, D_MODEL), lambda i: (i, 0)),
            pl.BlockSpec((1, D_MODEL), lambda i: (0, 0)),
            pl.BlockSpec((1, D_MODEL), lambda i: (0, 0)),
            pl.BlockSpec((None, N_EXPERTS, D_MODEL), lambda i: (layer, 0, 0)),
            pl.BlockSpec((None, N_EXPERTS, 1), lambda i: (layer, 0, 0)),
        ],
        out_specs=(
            pl.BlockSpec((tm * TOKEN_ROWS, LANES), lambda i: (i, 0)),
            pl.BlockSpec((TOP_K, tm), lambda i: (0, i)),
            pl.BlockSpec((TOP_K, tm), lambda i: (0, i)),
            pl.BlockSpec((TOP_K, tm), lambda i: (0, i)),
            pl.BlockSpec((N_EXPERTS, LANES), lambda i: (0, 0)),
        ),
        compiler_params=_params("arbitrary"),
        name="router",
    )(x, sc, sh, w_router_t, router_bias.reshape(DEPTH, N_EXPERTS, 1))


def _dest_kernel(pad_start_ref, eidx_ref, rank_ref, dest_ref):
    eidx = eidx_ref[...]
    dest = rank_ref[...]
    for e in range(N_EXPERTS):
        dest = dest + jnp.where(eidx == e, pad_start_ref[e], 0)
    dest_ref[...] = dest


def _dest_rows(pad_start, eidx, rank):
    k, t = eidx.shape
    grid_spec = pltpu.PrefetchScalarGridSpec(
        num_scalar_prefetch=1,
        grid=(1,),
        in_specs=[pl.BlockSpec((k, t), lambda i, p: (0, 0)), pl.BlockSpec((k, t), lambda i, p: (0, 0))],
        out_specs=pl.BlockSpec((k, t), lambda i, p: (0, 0)),
    )
    return pl.pallas_call(
        _dest_kernel,
        out_shape=jax.ShapeDtypeStruct((k, t), I32),
        grid_spec=grid_spec,
        compiler_params=_params("arbitrary"),
        name="dest_rows",
    )(pad_start, eidx, rank)


def _token_copy(src, src_tok, dst, dst_tok, sem):
    s = pl.ds(pl.multiple_of(src_tok * TOKEN_ROWS, TOKEN_ROWS), TOKEN_ROWS)
    d = pl.ds(pl.multiple_of(dst_tok * TOKEN_ROWS, TOKEN_ROWS), TOKEN_ROWS)
    return pltpu.make_async_copy(src.at[s, :], dst.at[d, :], sem)


def _block_copy(src, dst_hbm, blk, sem):
    rows = src.shape[0]
    return pltpu.make_async_copy(src, dst_hbm.at[pl.ds(pl.multiple_of(blk * rows, rows), rows), :], sem)


def _dispatch_kernel(dest_ref, zflag_ref, u_ref, xg_hbm, zbuf, sem, zsem, *, n_tok, n_blocks):
    i = pl.program_id(0)
    tm = u_ref.shape[0] // TOKEN_ROWS

    @pl.when(i == 0)
    def _():
        zbuf[...] = jnp.zeros_like(zbuf)

        def zero_start(b, _):
            @pl.when(zflag_ref[b] != 0)
            def _():
                _block_copy(zbuf, xg_hbm, b, zsem).start()
            return 0

        def zero_wait(b, _):
            @pl.when(zflag_ref[b] != 0)
            def _():
                _block_copy(zbuf, xg_hbm, b, zsem).wait()
            return 0

        lax.fori_loop(0, n_blocks, zero_start, 0)
        lax.fori_loop(0, n_blocks, zero_wait, 0)

    for k in range(TOP_K):
        base = k * n_tok + i * tm

        def issue(it, _, base=base):
            r0 = it * DMA_UNROLL
            for j in range(DMA_UNROLL):
                _token_copy(u_ref, r0 + j, xg_hbm, dest_ref[base + r0 + j], sem).start(priority=j % 2)
            return 0

        lax.fori_loop(0, tm // DMA_UNROLL, issue, 0)

    def drain(it, _):
        for _j in range(DMA_UNROLL):
            _token_copy(u_ref, 0, xg_hbm, 0, sem).wait()
        return 0

    lax.fori_loop(0, TOP_K * tm // DMA_UNROLL, drain, 0)


def _dispatch(dest_flat, zflag, u_packed, n_blocks):
    t = u_packed.shape[0] // TOKEN_ROWS
    tm = DISPATCH_TM
    grid_spec = pltpu.PrefetchScalarGridSpec(
        num_scalar_prefetch=2,
        grid=(t // tm,),
        in_specs=[pl.BlockSpec((tm * TOKEN_ROWS, LANES), lambda i, d, z: (i, 0))],
        out_specs=pl.BlockSpec(memory_space=pl.ANY),
        scratch_shapes=[
            pltpu.VMEM((EXPERT_BLOCK * TOKEN_ROWS, LANES), U32),
            pltpu.SemaphoreType.DMA(()),
            pltpu.SemaphoreType.DMA(()),
        ],
    )
    return pl.pallas_call(
        functools.partial(_dispatch_kernel, n_tok=t, n_blocks=n_blocks),
        out_shape=jax.ShapeDtypeStruct((n_blocks * EXPERT_BLOCK * TOKEN_ROWS, LANES), U32),
        grid_spec=grid_spec,
        compiler_params=_params("arbitrary"),
        name="dispatch",
    )(dest_flat, zflag, u_packed)


def _silu_gate(g, up):
    return (g * (1.0 / (1.0 + jnp.exp(-g))) * up).astype(BF16)


def _expert_kernel(be_ref, nused_ref, nxt_ref, slot_ref, x_ref, wg_hbm, wu_hbm, wd_hbm, o_ref,
                   wg_buf, wu_buf, wd_buf, wgu_bf, wd_bf, sems, *, layer):
    b = pl.program_id(0)

    def weight_copies(expert, slot):
        return (pltpu.make_async_copy(wg_hbm.at[layer, expert], wg_buf.at[slot], sems.at[slot]),
                pltpu.make_async_copy(wu_hbm.at[layer, expert], wu_buf.at[slot], sems.at[slot]),
                pltpu.make_async_copy(wd_hbm.at[layer, expert], wd_buf.at[slot], sems.at[slot]))

    @pl.when(b < nused_ref[0])
    def _():
        expert = be_ref[b]
        slot = slot_ref[b]

        @pl.when(b == 0)
        def _():
            for cp in weight_copies(expert, slot):
                cp.start()

        @pl.when(jnp.logical_or(b == 0, expert != be_ref[jnp.maximum(b - 1, 0)]))
        def _():
            nxt = nxt_ref[b]

            @pl.when(nxt >= 0)
            def _():
                for cp in weight_copies(nxt, 1 - slot):
                    cp.start()

            for cp in weight_copies(expert, slot):
                cp.wait()
            wgu_bf[:, :D_EXPERT] = wg_buf[slot].astype(BF16)
            wgu_bf[:, D_EXPERT:] = wu_buf[slot].astype(BF16)
            wd_bf[...] = wd_buf[slot].astype(BF16)

        for tok0 in range(0, EXPERT_BLOCK, EXPERT_CHUNK):
            lo, hi = _unpack_halves(_load_token_tiles(x_ref, EXPERT_CHUNK, tok0))
            x = jnp.concatenate([lo.astype(BF16), hi.astype(BF16)], axis=1)
            gu = jnp.dot(x, wgu_bf[...], preferred_element_type=F32)
            h = _silu_gate(gu[:, :D_EXPERT], gu[:, D_EXPERT:])
            out = jnp.dot(h, wd_bf[...], preferred_element_type=F32)
            _store_token_tiles(o_ref, _pack_halves(out.astype(BF16)), tok0)

    @pl.when(b >= nused_ref[0])
    def _():
        o_ref[...] = jnp.zeros_like(o_ref)


def _experts(blk_expert, n_used, blk_next, blk_slot, xg, w_gate, w_up, w_down, layer, n_blocks):
    blk_rows = EXPERT_BLOCK * TOKEN_ROWS
    hbm = lambda: pl.BlockSpec(memory_space=pl.ANY)
    grid_spec = pltpu.PrefetchScalarGridSpec(
        num_scalar_prefetch=4,
        grid=(n_blocks,),
        in_specs=[
            pl.BlockSpec((blk_rows, LANES), lambda b, be, nu, nx, sl: (jnp.minimum(b, nu[0] - 1), 0)),
            hbm(), hbm(), hbm(),
        ],
        out_specs=pl.BlockSpec((blk_rows, LANES), lambda b, be, nu, nx, sl: (b, 0)),
        scratch_shapes=[
            pltpu.VMEM((2, D_MODEL, D_EXPERT), F32),
            pltpu.VMEM((2, D_MODEL, D_EXPERT), F32),
            pltpu.VMEM((2, D_EXPERT, D_MODEL), F32),
            pltpu.VMEM((D_MODEL, 2 * D_EXPERT), BF16),
            pltpu.VMEM((D_EXPERT, D_MODEL), BF16),
            pltpu.SemaphoreType.DMA((2,)),
        ],
    )
    return pl.pallas_call(
        functools.partial(_expert_kernel, layer=layer),
        out_shape=jax.ShapeDtypeStruct((n_blocks * blk_rows, LANES), U32),
        grid_spec=grid_spec,
        compiler_params=_params("arbitrary"),
        name="experts",
    )(blk_expert, n_used, blk_next, blk_slot, xg, w_gate, w_up, w_down)


def _combine_kernel(dest_ref, eo_hbm, wts_ref, x_ref, sc_ref, sh_ref, g2_ref,
                    sg_ref, su_ref, sd_ref, lg_ref, lb_ref, o_ref, gbuf, sems, *, n_tok):
    i = pl.program_id(0)
    tm = x_ref.shape[0]
    half = D_MODEL // 2
    slot = i % 2

    def gather_tile(tile, into):
        for k in range(TOP_K):
            base = k * n_tok + tile * tm

            def issue(it, _, k=k, base=base):
                r0 = it * DMA_UNROLL
                for j in range(DMA_UNROLL):
                    _token_copy(eo_hbm, dest_ref[base + r0 + j], gbuf.at[into, k], r0 + j,
                                sems.at[into]).start(priority=j % 2)
                return 0

            lax.fori_loop(0, tm // DMA_UNROLL, issue, 0)

    @pl.when(i == 0)
    def _():
        gather_tile(0, 0)

    has_next = i + 1 < pl.num_programs(0)
    next_base = (i + 1) * tm

    def drain(it, _):
        for _j in range(DMA_UNROLL):
            _token_copy(eo_hbm, 0, gbuf.at[slot, 0], 0, sems.at[slot]).wait()
        return 0

    lax.fori_loop(0, TOP_K * tm // DMA_UNROLL, drain, 0)

    def routed_rows(rc, _):
        r0 = pl.multiple_of(rc * COMBINE_ROWS, COMBINE_ROWS)

        @pl.when(has_next)
        def _():
            for k in range(TOP_K):
                for j in range(COMBINE_ROWS):
                    src_tok = dest_ref[k * n_tok + next_base + r0 + j]
                    _token_copy(eo_hbm, src_tok, gbuf.at[1 - slot, k], r0 + j,
                                sems.at[1 - slot]).start(priority=j % 2)

        rows = pl.ds(r0, COMBINE_ROWS)
        wts = wts_ref[rows, :]
        for g in range(TOKEN_ROWS):
            acc_lo = acc_hi = None
            for k in range(TOP_K):
                words = gbuf[slot, k, pl.ds(r0 * TOKEN_ROWS + g, COMBINE_ROWS, stride=TOKEN_ROWS), :]
                lo, hi = _unpack_halves(words)
                w = wts[:, k:k + 1]
                acc_lo = w * lo if acc_lo is None else acc_lo + w * lo
                acc_hi = w * hi if acc_hi is None else acc_hi + w * hi
            o_ref[rows, g * LANES:(g + 1) * LANES] = acc_lo
            o_ref[rows, half + g * LANES:half + (g + 1) * LANES] = acc_hi
        return 0

    lax.fori_loop(0, tm // COMBINE_ROWS, routed_rows, 0)
    x = x_ref[...]
    u = (x * (1.0 + sc_ref[...]) + sh_ref[...]).astype(BF16)
    h = _silu_gate(jnp.dot(u, sg_ref[...], preferred_element_type=F32),
                   jnp.dot(u, su_ref[...], preferred_element_type=F32))
    shared = jnp.dot(h, sd_ref[...], preferred_element_type=F32)
    ffn = shared + o_ref[...]
    r = DEEPNORM_ALPHA * x + (1.0 + g2_ref[...]) * ffn
    o_ref[...] = _layer_norm(r, lg_ref[...], lb_ref[...])


def _combine(dest_flat, eo, wts_tk, x, sc, sh, g2, sg_bf, su_bf, sd_bf, ln_g, ln_b, layer):
    t = x.shape[0]
    tm = COMBINE_TM
    vec = lambda n: pl.BlockSpec((None, 1, n), lambda i, d: (layer, 0, 0))
    row = lambda: pl.BlockSpec((1, D_MODEL), lambda i, d: (0, 0))
    grid_spec = pltpu.PrefetchScalarGridSpec(
        num_scalar_prefetch=1,
        grid=(t // tm,),
        in_specs=[
            pl.BlockSpec(memory_space=pl.ANY),
            pl.BlockSpec((tm, TOP_K), lambda i, d: (i, 0)),
            pl.BlockSpec((tm, D_MODEL), lambda i, d: (i, 0)),
            row(), row(), row(),
            pl.BlockSpec((None, D_MODEL, D_EXPERT), lambda i, d: (layer, 0, 0)),
            pl.BlockSpec((None, D_MODEL, D_EXPERT), lambda i, d: (layer, 0, 0)),
            pl.BlockSpec((None, D_EXPERT, D_MODEL), lambda i, d: (layer, 0, 0)),
            vec(D_MODEL), vec(D_MODEL),
        ],
        out_specs=pl.BlockSpec((tm, D_MODEL), lambda i, d: (i, 0)),
        scratch_shapes=[
            pltpu.VMEM((2, TOP_K, tm * TOKEN_ROWS, LANES), U32),
            pltpu.SemaphoreType.DMA((2,)),
        ],
    )
    return pl.pallas_call(
        functools.partial(_combine_kernel, n_tok=t),
        out_shape=jax.ShapeDtypeStruct((t, D_MODEL), F32),
        grid_spec=grid_spec,
        compiler_params=_params("arbitrary"),
        name="combine_ln",
    )(dest_flat, eo, wts_tk, x, sc, sh, g2, sg_bf, su_bf, sd_bf,
      ln_g.reshape(DEPTH, 1, D_MODEL), ln_b.reshape(DEPTH, 1, D_MODEL))


def _block_plan(counts, n_blocks):
    blocks_per_e = (counts + EXPERT_BLOCK - 1) // EXPERT_BLOCK
    blk_end = jnp.cumsum(blocks_per_e)
    pad_start = ((blk_end - blocks_per_e) * EXPERT_BLOCK).astype(I32)
    n_used = blk_end[-1:].astype(I32)
    blk = jnp.arange(n_blocks, dtype=I32)[:, None]
    blk_expert = jnp.minimum(jnp.sum((blk_end[None, :] <= blk).astype(I32), axis=1), N_EXPERTS - 1)
    partial_blk = jnp.any((blk == blk_end[None, :] - 1) & (blocks_per_e[None, :] > 0), axis=1)
    zflag = (partial_blk | (blk[:, 0] >= n_used[0])).astype(I32)
    eid = jnp.arange(N_EXPERTS, dtype=I32)
    nonempty = blocks_per_e > 0
    parity = (jnp.cumsum(nonempty.astype(I32)) - 1) & 1
    later = nonempty[None, :] & (eid[None, :] > eid[:, None])
    next_e = jnp.min(jnp.where(later, eid[None, :], N_EXPERTS), axis=1)
    next_e = jnp.where(next_e == N_EXPERTS, -1, next_e)
    onehot = (blk_expert[:, None] == eid[None, :]).astype(I32)
    blk_slot = jnp.sum(onehot * parity[None, :], axis=1).astype(I32)
    blk_next = jnp.sum(onehot * next_e[None, :], axis=1).astype(I32)
    return pad_start, blk_expert.astype(I32), n_used, zflag, blk_next, blk_slot


def kernel(x, c, w_mod, b_mod, w_in, conv_w, g_conv, g_att, w_out, ln1_g, ln1_b, w_router,
           router_bias, w_gate, w_up, w_down, ws_gate, ws_up, ws_down, ln2_g, ln2_b):
    bsz, seq, d = x.shape
    assert bsz == 1 and d == D_MODEL
    t = bsz * seq
    n_blocks = -(-(t * TOP_K) // EXPERT_BLOCK) + N_EXPERTS
    xt = x.reshape(t, d)

    mod = _modulation(c, w_mod, b_mod)
    w_out_bf = w_out.astype(BF16)
    sg_bf, su_bf, sd_bf = ws_gate.astype(BF16), ws_up.astype(BF16), ws_down.astype(BF16)
    w_router_t = jnp.swapaxes(w_router, 1, 2)

    for l in range(DEPTH):
        sh1, sc1, g1, sh2, sc2, g2 = [mod[l, :, n * d:(n + 1) * d] for n in range(N_MOD)]
        hbc = _inproj(xt, sc1, sh1, w_in, l, 0, F32, 0)
        qkv = _inproj(xt, sc1, sh1, w_in, l, 3 * D_CONV, BF16, D_ATT // INPROJ_TN)
        yc = _conv(hbc, conv_w, g_conv, l)
        att = _attention(qkv)
        xt = _outproj(yc, att, g_att, w_out_bf, xt, g1, ln1_g, ln1_b, l)
        u_packed, eidx, wts, rank, cnt = _router(xt, sc2, sh2, w_router_t, router_bias, l)
        pad_start, blk_expert, n_used, zflag, blk_next, blk_slot = _block_plan(
            cnt[:, 0].astype(I32), n_blocks)
        dest = _dest_rows(pad_start, eidx, rank).reshape(-1)
        xg = _dispatch(dest, zflag, u_packed, n_blocks)
        eo = _experts(blk_expert, n_used, blk_next, blk_slot, xg, w_gate, w_up, w_down, l, n_blocks)
        xt = _combine(dest, eo, wts.T, xt, sc2, sh2, g2, sg_bf, su_bf, sd_bf, ln2_g, ln2_b, l)
    return xt.reshape(bsz, seq, d)
```

```python
import functools

import jax
import jax.numpy as jnp
from jax import lax
from jax.experimental import pallas as pl
from jax.experimental.pallas import tpu as pltpu

F32 = jnp.float32
BF16 = jnp.bfloat16
I32 = jnp.int32
U32 = jnp.uint32

D_MODEL = 2048
DEPTH = 4
D_CONV = D_MODEL // 2
D_ATT = D_MODEL // 2
HEAD_DIM = 128
N_HEADS = D_ATT // HEAD_DIM
N_EXPERTS = 64
TOP_K = 8
N_GROUPS = 8
GROUP_SIZE = N_EXPERTS // N_GROUPS
TOPK_GROUPS = 4
D_EXPERT = 384
ROUTED_SCALE = 2.5
EXPERT_BLOCK = 256
EXPERT_CHUNK = 128
DEEPNORM_ALPHA = (2 * DEPTH) ** 0.25
LN_EPS = 1e-5
RMS_EPS = 1e-6
N_MOD = 6
ATT_SCALE = HEAD_DIM ** -0.5
EXP_UNDERFLOW = -110.0

LANES = 128
SUBLANES = 8
VMEM_LIMIT = 48 * 1024 * 1024

MOD_TN = 1024
INPROJ_TM = 512
INPROJ_TN = 1024
CONV_TM = 512
ATT_BQ = 256
ATT_BK = 256
ATT_HEADS = 4
OUT_TM = 256
ROUTER_TM = 512
COMBINE_TM = 256
COMBINE_UNROLL = 2
STRIP = 16
DISPATCH_TM = 256
DMA_UNROLL = 8
TOKEN_ROWS = (D_MODEL // 2) // LANES


def _params(*sem):
    return pltpu.CompilerParams(dimension_semantics=sem, vmem_limit_bytes=VMEM_LIMIT)


def _mod_kernel(c_ref, w_ref, b_ref, o_ref):
    o_ref[...] = jnp.dot(c_ref[...], w_ref[...], preferred_element_type=F32) + b_ref[...]


def _modulation(c, w_mod, b_mod):
    n = N_MOD * D_MODEL
    c8 = jnp.broadcast_to(c, (SUBLANES, D_MODEL))
    out = pl.pallas_call(
        _mod_kernel,
        out_shape=jax.ShapeDtypeStruct((DEPTH, SUBLANES, n), F32),
        grid=(DEPTH, n // MOD_TN),
        in_specs=[
            pl.BlockSpec((SUBLANES, D_MODEL), lambda l, j: (0, 0)),
            pl.BlockSpec((None, D_MODEL, MOD_TN), lambda l, j: (l, 0, j)),
            pl.BlockSpec((None, 1, MOD_TN), lambda l, j: (l, 0, j)),
        ],
        out_specs=pl.BlockSpec((None, SUBLANES, MOD_TN), lambda l, j: (l, 0, j)),
        compiler_params=_params("arbitrary", "arbitrary"),
        name="modulation",
    )(c8, w_mod, b_mod.reshape(DEPTH, 1, n))
    return out[:, 0:1, :]


def _inproj_kernel(x_ref, sc_ref, sh_ref, w_ref, o_ref, w_bf, *, q_tiles):
    @pl.when(pl.program_id(1) == 0)
    def _():
        w_bf[...] = w_ref[...].astype(BF16)

    u = (x_ref[...] * (1.0 + sc_ref[...]) + sh_ref[...]).astype(BF16)
    acc = jnp.dot(u, w_bf[...], preferred_element_type=F32)
    if q_tiles:
        acc = acc * jnp.where(pl.program_id(0) < q_tiles, ATT_SCALE, 1.0)
    o_ref[...] = acc.astype(o_ref.dtype)


def _inproj(x, sc, sh, w_in, layer, col0, out_dtype, q_tiles):
    t = x.shape[0]
    n = 3 * D_CONV
    tile0 = col0 // INPROJ_TN
    return pl.pallas_call(
        functools.partial(_inproj_kernel, q_tiles=q_tiles),
        out_shape=jax.ShapeDtypeStruct((t, n), out_dtype),
        grid=(n // INPROJ_TN, t // INPROJ_TM),
        in_specs=[
            pl.BlockSpec((INPROJ_TM, D_MODEL), lambda j, i: (i, 0)),
            pl.BlockSpec((1, D_MODEL), lambda j, i: (0, 0)),
            pl.BlockSpec((1, D_MODEL), lambda j, i: (0, 0)),
            pl.BlockSpec((None, D_MODEL, INPROJ_TN), lambda j, i: (layer, 0, tile0 + j)),
        ],
        out_specs=pl.BlockSpec((INPROJ_TM, INPROJ_TN), lambda j, i: (i, j)),
        scratch_shapes=[pltpu.VMEM((D_MODEL, INPROJ_TN), BF16)],
        compiler_params=_params("arbitrary", "arbitrary"),
        name="inproj",
    )(x, sc, sh, w_in)


def _conv_kernel(h_ref, b_ref, c_ref, hh_ref, ch_ref, cw_ref, g_ref, o_ref):
    i = pl.program_id(0)
    v = c_ref[...] * h_ref[...]
    halo = jnp.where(i > 0, ch_ref[...] * hh_ref[...], 0.0)
    row = lax.broadcasted_iota(I32, v.shape, 0)
    prev1 = halo[SUBLANES - 1:SUBLANES, :]
    prev2 = halo[SUBLANES - 2:SUBLANES - 1, :]
    v1 = jnp.where(row == 0, prev1, pltpu.roll(v, 1, axis=0))
    v2 = jnp.where(row == 0, prev2, jnp.where(row == 1, prev1, pltpu.roll(v, 2, axis=0)))
    cw = cw_ref[...]
    conv = cw[0:1, :] * v2 + cw[1:2, :] * v1 + cw[2:3, :] * v
    y = b_ref[...] * conv
    ms = jnp.mean(y * y, axis=-1, keepdims=True)
    o_ref[...] = (y * lax.rsqrt(ms + RMS_EPS) * g_ref[...]).astype(o_ref.dtype)


def _conv(hbc, conv_w, g_conv, layer):
    t = hbc.shape[0]
    halo_blocks = CONV_TM // SUBLANES

    def halo_map(col):
        return lambda i: (jnp.maximum(i * halo_blocks - 1, 0), col)

    return pl.pallas_call(
        _conv_kernel,
        out_shape=jax.ShapeDtypeStruct((t, D_CONV), BF16),
        grid=(t // CONV_TM,),
        in_specs=[
            pl.BlockSpec((CONV_TM, D_CONV), lambda i: (i, 0)),
            pl.BlockSpec((CONV_TM, D_CONV), lambda i: (i, 1)),
            pl.BlockSpec((CONV_TM, D_CONV), lambda i: (i, 2)),
            pl.BlockSpec((SUBLANES, D_CONV), halo_map(0)),
            pl.BlockSpec((SUBLANES, D_CONV), halo_map(2)),
            pl.BlockSpec((None, 3, D_CONV), lambda i: (layer, 0, 0)),
            pl.BlockSpec((None, 1, D_CONV), lambda i: (layer, 0, 0)),
        ],
        out_specs=pl.BlockSpec((CONV_TM, D_CONV), lambda i: (i, 0)),
        compiler_params=_params("arbitrary"),
        name="conv_rms",
    )(hbc, hbc, hbc, hbc, hbc, conv_w, g_conv.reshape(DEPTH, 1, D_CONV))


def _attn_kernel(q_ref, k_ref, v_ref, o_ref, *, bq, bk, heads):
    i = pl.program_id(1)
    row = lax.broadcasted_iota(I32, (bq, bk), 0)
    col = lax.broadcasted_iota(I32, (bq, bk), 1)
    causal = col < row
    r2 = lax.broadcasted_iota(I32, (bk, bk + LANES), 0)
    c2 = lax.broadcasted_iota(I32, (bk, bk + LANES), 1)
    tri = jnp.where((r2 >= c2) | (c2 >= bk), 1.0, 0.0).astype(BF16)

    def block(h, j, later, diagonal):
        cols = slice(h * HEAD_DIM, (h + 1) * HEAD_DIM)
        start = pl.multiple_of(j * bk, bk)
        q = q_ref[:, cols]
        ks = k_ref[pl.ds(start, bk), cols]
        vs = v_ref[pl.ds(start, bk), cols]
        z = lax.dot_general(q, ks, (((1,), (1,)), ((), ())), preferred_element_type=F32)
        log_keep = -(jnp.maximum(z, 0.0) + jnp.log(1.0 + jnp.exp(-jnp.abs(z))))
        if diagonal:
            log_keep = jnp.where(causal, log_keep, 0.0)
        hi = log_keep.astype(BF16)
        lo = (log_keep - hi.astype(F32)).astype(BF16)
        sums = (jnp.dot(hi, tri, preferred_element_type=F32)
                + jnp.dot(lo, tri, preferred_element_type=F32))
        incl = sums[:, :bk] + jnp.concatenate([later] * (bk // LANES), axis=1)
        a = jnp.exp(z + incl)
        if diagonal:
            a = jnp.where(causal, a, 0.0)
        pv = jnp.dot(a.astype(BF16), vs, preferred_element_type=F32)
        return pv, later + sums[:, bk:]

    def any_live(laters):
        top = jnp.max(laters[0])
        for later in laters[1:]:
            top = jnp.maximum(top, jnp.max(later))
        return top > EXP_UNDERFLOW

    first = [block(h, i, jnp.zeros((bq, LANES), F32), True) for h in range(heads)]
    accs = tuple(pv for pv, _ in first)
    laters = tuple(later for _, later in first)

    def cond(c):
        jj, _, _, live = c
        return jnp.logical_and(jj <= i, live)

    def body(c):
        jj, accs, laters, _ = c
        step = [block(h, i - jj, laters[h], False) for h in range(heads)]
        accs = tuple(acc + pv for acc, (pv, _) in zip(accs, step))
        laters = tuple(later for _, later in step)
        return jj + 1, accs, laters, any_live(laters)

    _, accs, _, _ = lax.while_loop(cond, body, (jnp.int32(1), accs, laters, any_live(laters)))
    for h in range(heads):
        o_ref[:, h * HEAD_DIM:(h + 1) * HEAD_DIM] = accs[h]


def _attention(qkv):
    t = qkv.shape[0]
    bq, bk, heads = ATT_BQ, ATT_BK, ATT_HEADS
    assert bq == bk and HEAD_DIM == LANES
    groups = N_HEADS // heads
    width = heads * HEAD_DIM
    return pl.pallas_call(
        functools.partial(_attn_kernel, bq=bq, bk=bk, heads=heads),
        out_shape=jax.ShapeDtypeStruct((t, D_ATT), F32),
        grid=(groups, t // bq),
        in_specs=[
            pl.BlockSpec((bq, width), lambda h, i: (i, h)),
            pl.BlockSpec((t, width), lambda h, i: (0, groups + h)),
            pl.BlockSpec((t, width), lambda h, i: (0, 2 * groups + h)),
        ],
        out_specs=pl.BlockSpec((bq, width), lambda h, i: (i, h)),
        compiler_params=_params("arbitrary", "arbitrary"),
        name="stickbreak_attn",
    )(qkv, qkv, qkv)


def _layer_norm(r, g, b):
    mu = jnp.mean(r, axis=-1, keepdims=True)
    rc = r - mu
    var = jnp.mean(rc * rc, axis=-1, keepdims=True)
    return rc * lax.rsqrt(var + LN_EPS) * g + b


def _outproj_kernel(yc_ref, att_ref, ga_ref, wc_ref, wa_ref, x_ref, g1_ref, lg_ref, lb_ref, o_ref):
    att = att_ref[...]
    ms = jnp.mean(att * att, axis=-1, keepdims=True)
    ya = (att * lax.rsqrt(ms + RMS_EPS) * ga_ref[...]).astype(BF16)
    mix = (jnp.dot(yc_ref[...], wc_ref[...], preferred_element_type=F32)
           + jnp.dot(ya, wa_ref[...], preferred_element_type=F32))
    r = DEEPNORM_ALPHA * x_ref[...] + (1.0 + g1_ref[...]) * mix
    o_ref[...] = _layer_norm(r, lg_ref[...], lb_ref[...])


def _outproj(yc, att, g_att, w_out_bf, x, g1, ln_g, ln_b, layer):
    t = x.shape[0]
    vec = lambda n: pl.BlockSpec((None, 1, n), lambda i: (layer, 0, 0))
    return pl.pallas_call(
        _outproj_kernel,
        out_shape=jax.ShapeDtypeStruct((t, D_MODEL), F32),
        grid=(t // OUT_TM,),
        in_specs=[
            pl.BlockSpec((OUT_TM, D_CONV), lambda i: (i, 0)),
            pl.BlockSpec((OUT_TM, D_ATT), lambda i: (i, 0)),
            vec(D_ATT),
            pl.BlockSpec((None, D_CONV, D_MODEL), lambda i: (layer, 0, 0)),
            pl.BlockSpec((None, D_ATT, D_MODEL), lambda i: (layer, 1, 0)),
            pl.BlockSpec((OUT_TM, D_MODEL), lambda i: (i, 0)),
            pl.BlockSpec((1, D_MODEL), lambda i: (0, 0)),
            vec(D_MODEL),
            vec(D_MODEL),
        ],
        out_specs=pl.BlockSpec((OUT_TM, D_MODEL), lambda i: (i, 0)),
        compiler_params=_params("arbitrary"),
        name="outproj_ln",
    )(yc, att, g_att.reshape(DEPTH, 1, D_ATT), w_out_bf, w_out_bf, x, g1,
      ln_g.reshape(DEPTH, 1, D_MODEL), ln_b.reshape(DEPTH, 1, D_MODEL))


def _split_bf16(a):
    hi = a.astype(BF16)
    lo = (a - hi.astype(F32)).astype(BF16)
    return hi, lo


def _pack_halves(a_bf):
    n = a_bf.shape[1] // 2
    bits = lax.bitcast_convert_type(a_bf.astype(F32), U32)
    return (bits[:, :n] >> 16) | (bits[:, n:] & jnp.uint32(0xFFFF0000))


def _unpack_halves(words):
    lo = lax.bitcast_convert_type(words << 16, F32)
    hi = lax.bitcast_convert_type(words & jnp.uint32(0xFFFF0000), F32)
    return lo, hi


def _store_token_tiles(ref, words, tok0=0):
    m = words.shape[0]
    for g in range(TOKEN_ROWS):
        ref[pl.ds(tok0 * TOKEN_ROWS + g, m, stride=TOKEN_ROWS), :] = words[:, g * LANES:(g + 1) * LANES]


def _load_token_tiles(ref, m, tok0=0):
    return jnp.concatenate(
        [ref[pl.ds(tok0 * TOKEN_ROWS + g, m, stride=TOKEN_ROWS), :] for g in range(TOKEN_ROWS)], axis=1)


def _router_kernel(x_ref, sc_ref, sh_ref, wr_ref, bias_ref,
                   u_ref, eidx_ref, wts_ref, rank_ref, cnt_ref):
    step = pl.program_id(0)
    tm = x_ref.shape[0]
    neg_inf = -jnp.inf

    @pl.when(step == 0)
    def _():
        cnt_ref[...] = jnp.zeros_like(cnt_ref)

    u = x_ref[...] * (1.0 + sc_ref[...]) + sh_ref[...]
    u_hi, u_lo = _split_bf16(u)
    _store_token_tiles(u_ref, _pack_halves(u_hi))
    w_hi, w_lo = _split_bf16(wr_ref[...])
    nt = (((1,), (1,)), ((), ()))
    logits = (lax.dot_general(w_hi, u_hi, nt, preferred_element_type=F32)
              + lax.dot_general(w_hi, u_lo, nt, preferred_element_type=F32)
              + lax.dot_general(w_lo, u_hi, nt, preferred_element_type=F32))
    scores = 1.0 / (1.0 + jnp.exp(-logits))
    sel = scores + bias_ref[...]

    gscore = []
    for g in range(N_GROUPS):
        grp = sel[g * GROUP_SIZE:(g + 1) * GROUP_SIZE, :]
        m1 = jnp.max(grp, axis=0, keepdims=True)
        is_max = grp == m1
        n_max = jnp.sum(is_max.astype(F32), axis=0, keepdims=True)
        rest = jnp.max(jnp.where(is_max, neg_inf, grp), axis=0, keepdims=True)
        gscore.append(m1 + jnp.where(n_max >= 2.0, m1, rest))
    masked = []
    for g in range(N_GROUPS):
        beaten = jnp.zeros((1, tm), F32)
        for o in range(N_GROUPS):
            if o == g:
                continue
            wins = (gscore[o] >= gscore[g]) if o < g else (gscore[o] > gscore[g])
            beaten = beaten + wins.astype(F32)
        keep = beaten < float(TOPK_GROUPS)
        grp = sel[g * GROUP_SIZE:(g + 1) * GROUP_SIZE, :]
        masked.append(jnp.where(keep, grp, neg_inf))
    cand = jnp.concatenate(masked, axis=0)

    eiota = lax.broadcasted_iota(I32, (N_EXPERTS, tm), 0)
    picked = jnp.zeros((N_EXPERTS, tm), F32)
    idxs, ws = [], []
    for _ in range(TOP_K):
        m = jnp.max(cand, axis=0, keepdims=True)
        idx = jnp.min(jnp.where(cand == m, eiota, N_EXPERTS), axis=0, keepdims=True)
        hit = eiota == idx
        ws.append(jnp.sum(jnp.where(hit, scores, 0.0), axis=0, keepdims=True))
        idxs.append(idx)
        picked = jnp.where(hit, 1.0, picked)
        cand = jnp.where(hit, neg_inf, cand)
    wsum = ws[0]
    for k in range(1, TOP_K):
        wsum = wsum + ws[k]
    norm = ROUTED_SCALE / wsum

    r = lax.broadcasted_iota(I32, (tm, tm), 0)
    c = lax.broadcasted_iota(I32, (tm, tm), 1)
    before = jnp.where(r < c, 1.0, 0.0).astype(BF16)
    ones = jnp.ones((tm, LANES), BF16)
    picked_bf = picked.astype(BF16)
    prior = jnp.dot(picked_bf, before, preferred_element_type=F32)
    total = jnp.dot(picked_bf, ones, preferred_element_type=F32)
    carried = cnt_ref[...]
    rank_full = prior + jnp.concatenate([carried] * (tm // LANES), axis=1)
    cnt_ref[...] = carried + total

    for k in range(TOP_K):
        hit = eiota == idxs[k]
        eidx_ref[k:k + 1, :] = idxs[k]
        wts_ref[k:k + 1, :] = ws[k] * norm
        rank_ref[k:k + 1, :] = jnp.sum(jnp.where(hit, rank_full, 0.0), axis=0,
                                       keepdims=True).astype(I32)


def _router(x, sc, sh, w_router_t, router_bias, layer):
    t = x.shape[0]
    tm = ROUTER_TM
    return pl.pallas_call(
        _router_kernel,
        out_shape=(
            jax.ShapeDtypeStruct((t * TOKEN_ROWS, LANES), U32),
            jax.ShapeDtypeStruct((TOP_K, t), I32),
            jax.ShapeDtypeStruct((TOP_K, t), F32),
            jax.ShapeDtypeStruct((TOP_K, t), I32),
            jax.ShapeDtypeStruct((N_EXPERTS, LANES), F32),
        ),
        grid=(t // tm,),
        in_specs=[
            pl.BlockSpec((tm, D_MODEL), lambda i: (i, 0)),
            pl.BlockSpec((1, D_MODEL), lambda i: (0, 0)),
            pl.BlockSpec((1, D_MODEL), lambda i: (0, 0)),
            pl.BlockSpec((None, N_EXPERTS, D_MODEL), lambda i: (layer, 0, 0)),
            pl.BlockSpec((None, N_EXPERTS, 1), lambda i: (layer, 0, 0)),
        ],
        out_specs=(
            pl.BlockSpec((tm * TOKEN_ROWS, LANES), lambda i: (i, 0)),
            pl.BlockSpec((TOP_K, tm), lambda i: (0, i)),
            pl.BlockSpec((TOP_K, tm), lambda i: (0, i)),
            pl.BlockSpec((TOP_K, tm), lambda i: (0, i)),
            pl.BlockSpec((N_EXPERTS, LANES), lambda i: (0, 0)),
        ),
        compiler_params=_params("arbitrary"),
        name="router",
    )(x, sc, sh, w_router_t, router_bias.reshape(DEPTH, N_EXPERTS, 1))


def _dest_kernel(pad_start_ref, stage_base_ref, eidx_ref, rank_ref, dest_ref, srow_ref):
    i = pl.program_id(0)
    eidx = eidx_ref[...]
    dest = rank_ref[...]
    srow = rank_ref[...]
    for e in range(N_EXPERTS):
        hit = eidx == e
        dest = dest + jnp.where(hit, pad_start_ref[e], 0)
        srow = srow + jnp.where(hit, stage_base_ref[i * N_EXPERTS + e], 0)
    dest_ref[...] = dest
    srow_ref[...] = srow


def _dest_rows(pad_start, stage_base, eidx, rank):
    k, t = eidx.shape
    tm = COMBINE_TM
    blk = lambda: pl.BlockSpec((k, tm), lambda i, p, s: (0, i))
    grid_spec = pltpu.PrefetchScalarGridSpec(
        num_scalar_prefetch=2,
        grid=(t // tm,),
        in_specs=[blk(), blk()],
        out_specs=(blk(), blk()),
    )
    return pl.pallas_call(
        _dest_kernel,
        out_shape=(jax.ShapeDtypeStruct((k, t), I32), jax.ShapeDtypeStruct((k, t), I32)),
        grid_spec=grid_spec,
        compiler_params=_params("arbitrary"),
        name="dest_rows",
    )(pad_start, stage_base.reshape(-1), eidx, rank)


def _token_copy(src, src_tok, dst, dst_tok, sem):
    s = pl.ds(pl.multiple_of(src_tok * TOKEN_ROWS, TOKEN_ROWS), TOKEN_ROWS)
    d = pl.ds(pl.multiple_of(dst_tok * TOKEN_ROWS, TOKEN_ROWS), TOKEN_ROWS)
    return pltpu.make_async_copy(src.at[s, :], dst.at[d, :], sem)


def _block_copy(src, dst_hbm, blk, sem):
    rows = src.shape[0]
    return pltpu.make_async_copy(src, dst_hbm.at[pl.ds(pl.multiple_of(blk * rows, rows), rows), :], sem)


def _dispatch_kernel(dest_ref, zflag_ref, u_ref, xg_hbm, zbuf, sem, zsem, *, n_tok, n_blocks):
    i = pl.program_id(0)
    tm = u_ref.shape[0] // TOKEN_ROWS

    @pl.when(i == 0)
    def _():
        zbuf[...] = jnp.zeros_like(zbuf)

        def zero_start(b, _):
            @pl.when(zflag_ref[b] != 0)
            def _():
                _block_copy(zbuf, xg_hbm, b, zsem).start()
            return 0

        def zero_wait(b, _):
            @pl.when(zflag_ref[b] != 0)
            def _():
                _block_copy(zbuf, xg_hbm, b, zsem).wait()
            return 0

        lax.fori_loop(0, n_blocks, zero_start, 0)
        lax.fori_loop(0, n_blocks, zero_wait, 0)

    for k in range(TOP_K):
        base = k * n_tok + i * tm

        def issue(it, _, base=base):
            r0 = it * DMA_UNROLL
            for j in range(DMA_UNROLL):
                _token_copy(u_ref, r0 + j, xg_hbm, dest_ref[base + r0 + j], sem).start(priority=j % 2)
            return 0

        lax.fori_loop(0, tm // DMA_UNROLL, issue, 0)

    def drain(it, _):
        for _j in range(DMA_UNROLL):
            _token_copy(u_ref, 0, xg_hbm, 0, sem).wait()
        return 0

    lax.fori_loop(0, TOP_K * tm // DMA_UNROLL, drain, 0)


def _dispatch(dest_flat, zflag, u_packed, n_blocks):
    t = u_packed.shape[0] // TOKEN_ROWS
    tm = DISPATCH_TM
    grid_spec = pltpu.PrefetchScalarGridSpec(
        num_scalar_prefetch=2,
        grid=(t // tm,),
        in_specs=[pl.BlockSpec((tm * TOKEN_ROWS, LANES), lambda i, d, z: (i, 0))],
        out_specs=pl.BlockSpec(memory_space=pl.ANY),
        scratch_shapes=[
            pltpu.VMEM((EXPERT_BLOCK * TOKEN_ROWS, LANES), U32),
            pltpu.SemaphoreType.DMA(()),
            pltpu.SemaphoreType.DMA(()),
        ],
    )
    return pl.pallas_call(
        functools.partial(_dispatch_kernel, n_tok=t, n_blocks=n_blocks),
        out_shape=jax.ShapeDtypeStruct((n_blocks * EXPERT_BLOCK * TOKEN_ROWS, LANES), U32),
        grid_spec=grid_spec,
        compiler_params=_params("arbitrary"),
        name="dispatch",
    )(dest_flat, zflag, u_packed)


def _silu_gate(g, up):
    return (g * (1.0 / (1.0 + jnp.exp(-g))) * up).astype(BF16)


def _expert_kernel(be_ref, nused_ref, nxt_ref, slot_ref, x_ref, wg_hbm, wu_hbm, wd_hbm, o_ref,
                   wg_buf, wu_buf, wd_buf, wgu_bf, wd_bf, sems, *, layer):
    b = pl.program_id(0)

    def weight_copies(expert, slot):
        return (pltpu.make_async_copy(wg_hbm.at[layer, expert], wg_buf.at[slot], sems.at[slot]),
                pltpu.make_async_copy(wu_hbm.at[layer, expert], wu_buf.at[slot], sems.at[slot]),
                pltpu.make_async_copy(wd_hbm.at[layer, expert], wd_buf.at[slot], sems.at[slot]))

    @pl.when(b < nused_ref[0])
    def _():
        expert = be_ref[b]
        slot = slot_ref[b]

        @pl.when(b == 0)
        def _():
            for cp in weight_copies(expert, slot):
                cp.start()

        @pl.when(jnp.logical_or(b == 0, expert != be_ref[jnp.maximum(b - 1, 0)]))
        def _():
            nxt = nxt_ref[b]

            @pl.when(nxt >= 0)
            def _():
                for cp in weight_copies(nxt, 1 - slot):
                    cp.start()

            for cp in weight_copies(expert, slot):
                cp.wait()
            wgu_bf[:, :D_EXPERT] = wg_buf[slot].astype(BF16)
            wgu_bf[:, D_EXPERT:] = wu_buf[slot].astype(BF16)
            wd_bf[...] = wd_buf[slot].astype(BF16)

        for tok0 in range(0, EXPERT_BLOCK, EXPERT_CHUNK):
            lo, hi = _unpack_halves(_load_token_tiles(x_ref, EXPERT_CHUNK, tok0))
            x = jnp.concatenate([lo.astype(BF16), hi.astype(BF16)], axis=1)
            gu = jnp.dot(x, wgu_bf[...], preferred_element_type=F32)
            h = _silu_gate(gu[:, :D_EXPERT], gu[:, D_EXPERT:])
            out = jnp.dot(h, wd_bf[...], preferred_element_type=F32)
            _store_token_tiles(o_ref, _pack_halves(out.astype(BF16)), tok0)

    @pl.when(b >= nused_ref[0])
    def _():
        o_ref[...] = jnp.zeros_like(o_ref)


def _experts(blk_expert, n_used, blk_next, blk_slot, xg, w_gate, w_up, w_down, layer, n_blocks):
    blk_rows = EXPERT_BLOCK * TOKEN_ROWS
    hbm = lambda: pl.BlockSpec(memory_space=pl.ANY)
    grid_spec = pltpu.PrefetchScalarGridSpec(
        num_scalar_prefetch=4,
        grid=(n_blocks,),
        in_specs=[
            pl.BlockSpec((blk_rows, LANES), lambda b, be, nu, nx, sl: (jnp.minimum(b, nu[0] - 1), 0)),
            hbm(), hbm(), hbm(),
        ],
        out_specs=pl.BlockSpec((blk_rows, LANES), lambda b, be, nu, nx, sl: (b, 0)),
        scratch_shapes=[
            pltpu.VMEM((2, D_MODEL, D_EXPERT), F32),
            pltpu.VMEM((2, D_MODEL, D_EXPERT), F32),
            pltpu.VMEM((2, D_EXPERT, D_MODEL), F32),
            pltpu.VMEM((D_MODEL, 2 * D_EXPERT), BF16),
            pltpu.VMEM((D_EXPERT, D_MODEL), BF16),
            pltpu.SemaphoreType.DMA((2,)),
        ],
    )
    return pl.pallas_call(
        functools.partial(_expert_kernel, layer=layer),
        out_shape=jax.ShapeDtypeStruct((n_blocks * blk_rows, LANES), U32),
        grid_spec=grid_spec,
        compiler_params=_params("arbitrary"),
        name="experts",
    )(blk_expert, n_used, blk_next, blk_slot, xg, w_gate, w_up, w_down)


def _strip_copy(eo_hbm, src_tok, stage, dst_tok, sem):
    rows = STRIP * TOKEN_ROWS
    s = pl.ds(pl.multiple_of(src_tok * TOKEN_ROWS, TOKEN_ROWS), rows)
    d = pl.ds(pl.multiple_of(dst_tok * TOKEN_ROWS, TOKEN_ROWS), rows)
    return pltpu.make_async_copy(eo_hbm.at[s, :], stage.at[d, :], sem)


def _combine_kernel(srow_ref, w_ref, strip_src_ref, strip_dst_ref, strip_len_ref, eo_hbm,
                    x_ref, sc_ref, sh_ref, g2_ref, sg_ref, su_ref, sd_ref, lg_ref, lb_ref,
                    o_ref, stage, sum_lo, sum_hi, sem, *, n_tok):
    i = pl.program_id(0)
    tm = x_ref.shape[0]

    def for_each_strip_copy(fn):
        def per_expert(e, _):
            idx = i * N_EXPERTS + e
            src0 = strip_src_ref[idx]
            dst0 = strip_dst_ref[idx]

            def per_chunk(c, _):
                fn(_strip_copy(eo_hbm, src0 + c * STRIP, stage, dst0 + c * STRIP, sem))
                return 0

            lax.fori_loop(0, strip_len_ref[idx], per_chunk, 0)
            return 0

        lax.fori_loop(0, N_EXPERTS, per_expert, 0)

    for_each_strip_copy(lambda cp: cp.start())

    x = x_ref[...]
    u = (x * (1.0 + sc_ref[...]) + sh_ref[...]).astype(BF16)
    h = _silu_gate(jnp.dot(u, sg_ref[...], preferred_element_type=F32),
                   jnp.dot(u, su_ref[...], preferred_element_type=F32))
    shared = jnp.dot(h, sd_ref[...], preferred_element_type=F32)

    for_each_strip_copy(lambda cp: cp.wait())

    def routed_tokens(it, _):
        for j in range(COMBINE_UNROLL):
            r = it * COMBINE_UNROLL + j
            acc_lo = acc_hi = None
            for k in range(TOP_K):
                idx = k * n_tok + i * tm + r
                src = pl.ds(pl.multiple_of(srow_ref[idx] * TOKEN_ROWS, TOKEN_ROWS), TOKEN_ROWS)
                lo, hi = _unpack_halves(stage[src, :])
                w = w_ref[idx]
                acc_lo = w * lo if acc_lo is None else acc_lo + w * lo
                acc_hi = w * hi if acc_hi is None else acc_hi + w * hi
            dst = pl.ds(pl.multiple_of(r * TOKEN_ROWS, TOKEN_ROWS), TOKEN_ROWS)
            sum_lo[dst, :] = acc_lo
            sum_hi[dst, :] = acc_hi
        return 0

    lax.fori_loop(0, tm // COMBINE_UNROLL, routed_tokens, 0)
    routed = jnp.concatenate([_load_token_tiles(sum_lo, tm), _load_token_tiles(sum_hi, tm)], axis=1)
    ffn = shared + routed
    r = DEEPNORM_ALPHA * x + (1.0 + g2_ref[...]) * ffn
    o_ref[...] = _layer_norm(r, lg_ref[...], lb_ref[...])


def _combine(srow_flat, wts_flat, strips, eo, x, sc, sh, g2, sg_bf, su_bf, sd_bf, ln_g, ln_b, layer):
    t = x.shape[0]
    tm = COMBINE_TM
    vec = lambda n: pl.BlockSpec((None, 1, n), lambda i, *_: (layer, 0, 0))
    row = lambda: pl.BlockSpec((1, D_MODEL), lambda i, *_: (0, 0))
    stage_tokens = TOP_K * tm + N_EXPERTS * (STRIP - 1)
    grid_spec = pltpu.PrefetchScalarGridSpec(
        num_scalar_prefetch=5,
        grid=(t // tm,),
        in_specs=[
            pl.BlockSpec(memory_space=pl.ANY),
            pl.BlockSpec((tm, D_MODEL), lambda i, *_: (i, 0)),
            row(), row(), row(),
            pl.BlockSpec((None, D_MODEL, D_EXPERT), lambda i, *_: (layer, 0, 0)),
            pl.BlockSpec((None, D_MODEL, D_EXPERT), lambda i, *_: (layer, 0, 0)),
            pl.BlockSpec((None, D_EXPERT, D_MODEL), lambda i, *_: (layer, 0, 0)),
            vec(D_MODEL), vec(D_MODEL),
        ],
        out_specs=pl.BlockSpec((tm, D_MODEL), lambda i, *_: (i, 0)),
        scratch_shapes=[
            pltpu.VMEM((stage_tokens * TOKEN_ROWS, LANES), U32),
            pltpu.VMEM((tm * TOKEN_ROWS, LANES), F32),
            pltpu.VMEM((tm * TOKEN_ROWS, LANES), F32),
            pltpu.SemaphoreType.DMA(()),
        ],
    )
    strip_src, strip_dst, strip_len = strips
    return pl.pallas_call(
        functools.partial(_combine_kernel, n_tok=t),
        out_shape=jax.ShapeDtypeStruct((t, D_MODEL), F32),
        grid_spec=grid_spec,
        compiler_params=_params("arbitrary"),
        name="combine_ln",
    )(srow_flat, wts_flat, strip_src, strip_dst, strip_len, eo, x, sc, sh, g2, sg_bf, su_bf, sd_bf,
      ln_g.reshape(DEPTH, 1, D_MODEL), ln_b.reshape(DEPTH, 1, D_MODEL))


def _combine_plan(eidx, pad_start, n_tok):
    n_tiles = n_tok // COMBINE_TM
    eid = jnp.arange(N_EXPERTS, dtype=I32)
    onehot = eidx.reshape(TOP_K, n_tiles, COMBINE_TM)[..., None] == eid
    hist = jnp.sum(onehot.astype(I32), axis=(0, 2))
    before = jnp.cumsum(hist, axis=0) - hist
    strip_src = pad_start[None, :] + before
    strip_len = (hist + STRIP - 1) // STRIP
    strip_dst = (jnp.cumsum(strip_len, axis=1) - strip_len) * STRIP
    stage_base = strip_dst - before
    flat = lambda a: a.reshape(-1).astype(I32)
    return (flat(strip_src), flat(strip_dst), flat(strip_len)), stage_base.astype(I32)


def _block_plan(counts, n_blocks):
    blocks_per_e = (counts + EXPERT_BLOCK - 1) // EXPERT_BLOCK
    blk_end = jnp.cumsum(blocks_per_e)
    pad_start = ((blk_end - blocks_per_e) * EXPERT_BLOCK).astype(I32)
    n_used = blk_end[-1:].astype(I32)
    blk = jnp.arange(n_blocks, dtype=I32)[:, None]
    blk_expert = jnp.minimum(jnp.sum((blk_end[None, :] <= blk).astype(I32), axis=1), N_EXPERTS - 1)
    partial_blk = jnp.any((blk == blk_end[None, :] - 1) & (blocks_per_e[None, :] > 0), axis=1)
    zflag = (partial_blk | (blk[:, 0] >= n_used[0])).astype(I32)
    eid = jnp.arange(N_EXPERTS, dtype=I32)
    nonempty = blocks_per_e > 0
    parity = (jnp.cumsum(nonempty.astype(I32)) - 1) & 1
    later = nonempty[None, :] & (eid[None, :] > eid[:, None])
    next_e = jnp.min(jnp.where(later, eid[None, :], N_EXPERTS), axis=1)
    next_e = jnp.where(next_e == N_EXPERTS, -1, next_e)
    onehot = (blk_expert[:, None] == eid[None, :]).astype(I32)
    blk_slot = jnp.sum(onehot * parity[None, :], axis=1).astype(I32)
    blk_next = jnp.sum(onehot * next_e[None, :], axis=1).astype(I32)
    return pad_start, blk_expert.astype(I32), n_used, zflag, blk_next, blk_slot


def kernel(x, c, w_mod, b_mod, w_in, conv_w, g_conv, g_att, w_out, ln1_g, ln1_b, w_router,
           router_bias, w_gate, w_up, w_down, ws_gate, ws_up, ws_down, ln2_g, ln2_b):
    bsz, seq, d = x.shape
    assert bsz == 1 and d == D_MODEL
    t = bsz * seq
    n_blocks = -(-(t * TOP_K) // EXPERT_BLOCK) + N_EXPERTS + 1
    xt = x.reshape(t, d)

    mod = _modulation(c, w_mod, b_mod)
    w_out_bf = w_out.astype(BF16)
    sg_bf, su_bf, sd_bf = ws_gate.astype(BF16), ws_up.astype(BF16), ws_down.astype(BF16)
    w_router_t = jnp.swapaxes(w_router, 1, 2)

    for l in range(DEPTH):
        sh1, sc1, g1, sh2, sc2, g2 = [mod[l, :, n * d:(n + 1) * d] for n in range(N_MOD)]
        hbc = _inproj(xt, sc1, sh1, w_in, l, 0, F32, 0)
        qkv = _inproj(xt, sc1, sh1, w_in, l, 3 * D_CONV, BF16, D_ATT // INPROJ_TN)
        yc = _conv(hbc, conv_w, g_conv, l)
        att = _attention(qkv)
        xt = _outproj(yc, att, g_att, w_out_bf, xt, g1, ln1_g, ln1_b, l)
        u_packed, eidx, wts, rank, cnt = _router(xt, sc2, sh2, w_router_t, router_bias, l)
        pad_start, blk_expert, n_used, zflag, blk_next, blk_slot = _block_plan(
            cnt[:, 0].astype(I32), n_blocks)
        strips, stage_base = _combine_plan(eidx, pad_start, t)
        dest, srow = _dest_rows(pad_start, stage_base, eidx, rank)
        xg = _dispatch(dest.reshape(-1), zflag, u_packed, n_blocks)
        eo = _experts(blk_expert, n_used, blk_next, blk_slot, xg, w_gate, w_up, w_down, l, n_blocks)
        xt = _combine(srow.reshape(-1), wts.reshape(-1), strips, eo, xt, sc2, sh2, g2,
                      sg_bf, su_bf, sd_bf, ln2_g, ln2_b, l)
    return xt.reshape(bsz, seq, d)
```
